```python
import math
import jax, jax.numpy as jnp
from jax import lax
import numpy as np

D_MODEL = 1024
BATCH = 4
SEQ = 8192
DEPTH = 1

GRID_W = 64
HEAD_DIM = 64
A_HEADS = D_MODEL // (2 * HEAD_DIM)
A_KV_HEADS = A_HEADS // 4
B_HEADS = D_MODEL // (2 * HEAD_DIM)
A_WIDTH = A_HEADS * HEAD_DIM
A_KV_WIDTH = A_KV_HEADS * HEAD_DIM
B_WIDTH = B_HEADS * HEAD_DIM
MIX_WIDTH = A_WIDTH + B_WIDTH
IN_WIDTH = A_WIDTH + 2 * A_KV_WIDTH + 3 * B_WIDTH
Q_BLOCK = 128
ROPE_THETA = 10000.0
NA_KH = 8
NA_KW = 16
PEER_HEADS = 8
PEER_N_KEYS = 128
PEER_N_EXPERTS = PEER_N_KEYS * PEER_N_KEYS
PEER_QUERY_DIM = 256
PEER_SUB_DIM = PEER_QUERY_DIM // 2
PEER_TOPK = 16
PEER_BLOCK = 128
EPS = 1e-6

kernel_name = "hymba_gqa_natten_peer_block"


def rmsnorm(x, g):
    xf = x.astype(jnp.float32)
    y = xf * lax.rsqrt(jnp.mean(xf * xf, axis=-1, keepdims=True) + EPS)
    return (y * g.astype(jnp.float32)).astype(x.dtype)


def rope_axis(x, pos):
    half = x.shape[-1] // 2
    freqs = ROPE_THETA ** (-jnp.arange(half, dtype=jnp.float32) / half)
    ang = pos.astype(jnp.float32)[:, None] * freqs[None, :]
    cos = jnp.cos(ang)[None, :, None, :]
    sin = jnp.sin(ang)[None, :, None, :]
    xf = x.astype(jnp.float32)
    x1, x2 = xf[..., :half], xf[..., half:]
    out = jnp.concatenate([x1 * cos - x2 * sin, x1 * sin + x2 * cos], axis=-1)
    return out.astype(x.dtype)


def axial_rope(x, row, col):
    d = x.shape[-1] // 2
    return jnp.concatenate([rope_axis(x[..., :d], row), rope_axis(x[..., d:], col)], axis=-1)


def gqa_axial_attention(q, k, v, row, col, q_norm_g, k_norm_g):
    bsz, seq, nh, dh = q.shape
    grp = nh // A_KV_HEADS
    q = axial_rope(rmsnorm(q, q_norm_g), row, col)
    k = axial_rope(rmsnorm(k, k_norm_g), row, col)
    nblk = seq // Q_BLOCK
    qb = q.reshape(bsz, nblk, Q_BLOCK, A_KV_HEADS, grp, dh).transpose(1, 0, 2, 3, 4, 5)
    scale = dh ** -0.5

    def block(qblk):
        s = jnp.einsum('bqkgd,bskd->bkgqs', qblk, k).astype(jnp.float32) * scale
        p = jax.nn.softmax(s, axis=-1).astype(v.dtype)
        return jnp.einsum('bkgqs,bskd->bqkgd', p, v)

    o = lax.map(block, qb)
    return o.transpose(1, 0, 2, 3, 4, 5).reshape(bsz, seq, nh * dh)


def natten_tables(seq):
    rows = seq // GRID_W
    kh = min(NA_KH, rows)
    t = jnp.arange(seq, dtype=jnp.int32)
    r = t // GRID_W
    cc = t % GRID_W
    rs = jnp.clip(r - kh // 2, 0, rows - kh)
    cs = jnp.clip(cc - NA_KW // 2, 0, GRID_W - NA_KW)
    key_r = rs[:, None, None] + jnp.arange(kh, dtype=jnp.int32)[None, :, None]
    key_c = cs[:, None, None] + jnp.arange(NA_KW, dtype=jnp.int32)[None, None, :]
    shape = (seq, kh, NA_KW)
    idx = jnp.broadcast_to(key_r * GRID_W + key_c, shape).reshape(seq, kh * NA_KW)
    dri = jnp.broadcast_to(key_r - r[:, None, None] + (NA_KH - 1), shape).reshape(seq, kh * NA_KW)
    dci = jnp.broadcast_to(key_c - cc[:, None, None] + (NA_KW - 1), shape).reshape(seq, kh * NA_KW)
    return idx, dri, dci


def neighbourhood_attention(q, k, v, rpb):
    bsz, seq, nh, dh = q.shape
    idx, dri, dci = natten_tables(seq)
    nkeys = idx.shape[-1]
    nblk = seq // Q_BLOCK
    qb = q.reshape(bsz, nblk, Q_BLOCK, nh, dh).swapaxes(0, 1)
    idxb = idx.reshape(nblk, Q_BLOCK, nkeys)
    drb = dri.reshape(nblk, Q_BLOCK, nkeys)
    dcb = dci.reshape(nblk, Q_BLOCK, nkeys)
    scale = dh ** -0.5

    def block(args):
        qblk, ib, rb, cb = args
        kn = k[:, ib]
        vn = v[:, ib]
        bias = rpb[:, rb, cb].astype(jnp.float32)
        s = jnp.einsum('bqhd,bqlhd->bhql', qblk, kn).astype(jnp.float32) * scale + bias[None]
        p = jax.nn.softmax(s, axis=-1).astype(v.dtype)
        return jnp.einsum('bhql,bqlhd->bqhd', p, vn)

    o = lax.map(block, (qb, idxb, drb, dcb))
    return o.swapaxes(0, 1).reshape(bsz, seq, nh * dh)


def peer_ffn(h, w_query, sub_keys_1, sub_keys_2, peer_u, peer_v):
    bsz, seq, d = h.shape
    nblk = seq // PEER_BLOCK
    hb = h.reshape(bsz, nblk, PEER_BLOCK, d).swapaxes(0, 1)

    def block(hblk):
        q = (hblk @ w_query).reshape(bsz, PEER_BLOCK, PEER_HEADS, 2, PEER_SUB_DIM)
        s1 = jnp.einsum('bthd,nd->bthn', q[..., 0, :], sub_keys_1).astype(jnp.float32)
        s2 = jnp.einsum('bthd,nd->bthn', q[..., 1, :], sub_keys_2).astype(jnp.float32)
        v1, i1 = lax.top_k(s1, PEER_TOPK)
        v2, i2 = lax.top_k(s2, PEER_TOPK)
        cand = (v1[..., :, None] + v2[..., None, :]).reshape(bsz, PEER_BLOCK, PEER_HEADS, PEER_TOPK * PEER_TOPK)
        cidx = (i1[..., :, None] * PEER_N_KEYS + i2[..., None, :]).reshape(bsz, PEER_BLOCK, PEER_HEADS, PEER_TOPK * PEER_TOPK)
        top, pos = lax.top_k(cand, PEER_TOPK)
        eidx = jnp.take_along_axis(cidx, pos, axis=-1)
        g = jax.nn.softmax(top, axis=-1).astype(h.dtype)
        u = peer_u[eidx]
        a = jax.nn.gelu(jnp.einsum('bthkd,btd->bthk', u, hblk), approximate=False)
        vv = peer_v[eidx]
        return jnp.einsum('bthk,bthkd->btd', g * a, vv)

    o = lax.map(block, hb)
    return o.swapaxes(0, 1).reshape(bsz, seq, d)


def setup_inputs(seed: int = 0) -> dict:
    key = jax.random.key(seed)
    ks = jax.random.split(key, 20)
    f32 = jnp.float32
    nrm = lambda k, shape, s: jax.random.normal(k, shape, f32) * s
    gain = lambda k, shape: jnp.ones(shape, f32) + 0.01 * jax.random.normal(k, shape, f32)
    return {
        "x": nrm(ks[0], (BATCH, SEQ, D_MODEL), 1.0),
        "c": nrm(ks[1], (BATCH, D_MODEL), 1.0),
        "w_ada": nrm(ks[2], (DEPTH, D_MODEL, 6 * D_MODEL), D_MODEL ** -0.5),
        "b_ada": nrm(ks[3], (DEPTH, 6 * D_MODEL), 0.01),
        "norm1_g": gain(ks[4], (DEPTH, D_MODEL)),
        "w_in": nrm(ks[5], (DEPTH, D_MODEL, IN_WIDTH), D_MODEL ** -0.5),
        "q_norm_g": gain(ks[6], (DEPTH, HEAD_DIM)),
        "k_norm_g": gain(ks[7], (DEPTH, HEAD_DIM)),
        "natten_rpb": nrm(ks[8], (DEPTH, B_HEADS, 2 * NA_KH - 1, 2 * NA_KW - 1), 0.1),
        "group_norm_a_g": gain(ks[9], (DEPTH, A_WIDTH)),
        "group_norm_b_g": gain(ks[10], (DEPTH, B_WIDTH)),
        "w_out": nrm(ks[11], (DEPTH, MIX_WIDTH, D_MODEL), MIX_WIDTH ** -0.5),
        "norm2_g": gain(ks[12], (DEPTH, D_MODEL)),
        "peer_w_query": nrm(ks[13], (DEPTH, D_MODEL, PEER_HEADS * PEER_QUERY_DIM), D_MODEL ** -0.5),
        "peer_sub_keys_1": nrm(ks[14], (DEPTH, PEER_N_KEYS, PEER_SUB_DIM), PEER_SUB_DIM ** -0.5),
        "peer_sub_keys_2": nrm(ks[15], (DEPTH, PEER_N_KEYS, PEER_SUB_DIM), PEER_SUB_DIM ** -0.5),
        "peer_u": nrm(ks[16], (DEPTH, PEER_N_EXPERTS, D_MODEL), D_MODEL ** -0.5),
        "peer_v": nrm(ks[17], (DEPTH, PEER_N_EXPERTS, D_MODEL), PEER_HEADS ** -0.5),
        "final_norm_g": gain(ks[18], (D_MODEL,)),
    }


def reference(x, c, w_ada, b_ada, norm1_g, w_in, q_norm_g, k_norm_g, natten_rpb,
              group_norm_a_g, group_norm_b_g, w_out, norm2_g, peer_w_query,
              peer_sub_keys_1, peer_sub_keys_2, peer_u, peer_v, final_norm_g):
    bsz, seq, d = x.shape
    t = jnp.arange(seq, dtype=jnp.int32)
    row = t // GRID_W
    col = t % GRID_W
    o1 = A_WIDTH
    o2 = o1 + A_KV_WIDTH
    o3 = o2 + A_KV_WIDTH
    o4 = o3 + B_WIDTH
    o5 = o4 + B_WIDTH
    for l in range(DEPTH):
        mod = jax.nn.silu(c) @ w_ada[l] + b_ada[l]
        shift1, scale1, gate1, shift2, scale2, gate2 = jnp.split(mod, 6, axis=-1)
        h = rmsnorm(x, norm1_g[l]) * (1.0 + scale1[:, None, :]) + shift1[:, None, :]
        p = h @ w_in[l]
        qa = p[..., :o1].reshape(bsz, seq, A_HEADS, HEAD_DIM)
        ka = p[..., o1:o2].reshape(bsz, seq, A_KV_HEADS, HEAD_DIM)
        va = p[..., o2:o3].reshape(bsz, seq, A_KV_HEADS, HEAD_DIM)
        qb = p[..., o3:o4].reshape(bsz, seq, B_HEADS, HEAD_DIM)
        kb = p[..., o4:o5].reshape(bsz, seq, B_HEADS, HEAD_DIM)
        vb = p[..., o5:].reshape(bsz, seq, B_HEADS, HEAD_DIM)
        out_a = gqa_axial_attention(qa, ka, va, row, col, q_norm_g[l], k_norm_g[l])
        out_b = neighbourhood_attention(qb, kb, vb, natten_rpb[l])
        mix = jnp.concatenate([rmsnorm(out_a, group_norm_a_g[l]),
                               rmsnorm(out_b, group_norm_b_g[l])], axis=-1) @ w_out[l]
        x = x + gate1[:, None, :] * mix
        h2 = rmsnorm(x, norm2_g[l]) * (1.0 + scale2[:, None, :]) + shift2[:, None, :]
        ffn = peer_ffn(h2, peer_w_query[l], peer_sub_keys_1[l], peer_sub_keys_2[l], peer_u[l], peer_v[l])
        x = x + gate2[:, None, :] * ffn
    return rmsnorm(x, final_norm_g)
```

```python
import functools

import numpy as np
import jax
import jax.numpy as jnp
from jax import lax
from jax.experimental import pallas as pl
from jax.experimental.pallas import tpu as pltpu

F32 = jnp.float32
BF16 = jnp.bfloat16

GRID_W = 64
HEAD_DIM = 64
A_HEADS = 8
A_KV_HEADS = 2
B_HEADS = 8
A_WIDTH = A_HEADS * HEAD_DIM
A_KV_WIDTH = A_KV_HEADS * HEAD_DIM
B_WIDTH = B_HEADS * HEAD_DIM
QK_WIDTH = A_WIDTH + A_KV_WIDTH
ROPE_THETA = 10000.0
NA_KH = 8
NA_KW = 16
PEER_HEADS = 8
PEER_N_KEYS = 128
PEER_SUB_DIM = 128
PEER_TOPK = 16
PEER_SLOTS = PEER_HEADS * PEER_TOPK
EPS = 1e-6

LANES = 128
Q_BLOCK = 128
NA_SLAB_BLOCKS = 5
NEG_BIG = -1e30
VMEM_LIMIT = 48 * 1024 * 1024


def _cparams(sem):
    return pltpu.CompilerParams(dimension_semantics=sem, vmem_limit_bytes=VMEM_LIMIT)


def _ada_kernel(c_ref, w_ref, b_ref, o_ref):
    c = c_ref[...]
    sc = c / (1.0 + jnp.exp(-c))
    o_ref[...] = jnp.dot(sc, w_ref[...], preferred_element_type=F32) + b_ref[...]


def _ada(c8, w_ada, b_ada):
    rows, d = c8.shape
    n = w_ada.shape[1]
    tn = 512
    return pl.pallas_call(
        _ada_kernel,
        grid=(n // tn,),
        in_specs=[pl.BlockSpec((rows, d), lambda j: (0, 0)),
                  pl.BlockSpec((d, tn), lambda j: (0, j)),
                  pl.BlockSpec((1, tn), lambda j: (0, j))],
        out_specs=pl.BlockSpec((rows, tn), lambda j: (0, j)),
        out_shape=jax.ShapeDtypeStruct((rows, n), F32),
        compiler_params=_cparams(("arbitrary",)),
        name="ada",
    )(c8, w_ada, b_ada.reshape(1, n))


def _inproj_kernel(x_ref, sh_ref, sc_ref, g1_ref, w_ref, bd_ref, gqk_ref, cos_ref, sin_ref,
                   qa_ref, kx_ref, vx_ref, qlo_ref, qhi_ref, kb_ref, vb_ref):
    x = x_ref[0]
    ms = jnp.mean(x * x, axis=-1, keepdims=True)
    h = x * lax.rsqrt(ms + EPS) * g1_ref[...]
    h = h * (1.0 + sc_ref[0]) + sh_ref[0]
    p = jnp.dot(h.astype(BF16), w_ref[...], preferred_element_type=F32)
    tm = p.shape[0]

    lane = lax.broadcasted_iota(jnp.int32, (tm, LANES), 1)
    lo = lane < HEAD_DIM
    first16 = (lane % 32) < 16

    pq = p[:, :QK_WIDTH]
    ss = jnp.dot(pq * pq, bd_ref[...], preferred_element_type=F32)
    qn = pq * lax.rsqrt(ss * (1.0 / HEAD_DIM) + EPS) * gqk_ref[...]
    cos = cos_ref[...]
    sin = sin_ref[...]

    def rope(cn):
        partner = jnp.where(first16, pltpu.roll(cn, LANES - 16, 1), pltpu.roll(cn, 16, 1))
        return cn * cos + partner * sin

    for k in range(A_WIDTH // LANES):
        qa_ref[0, :, k * LANES:(k + 1) * LANES] = rope(qn[:, k * LANES:(k + 1) * LANES]).astype(BF16)
    kk = rope(qn[:, A_WIDTH:QK_WIDTH])
    kr = pltpu.roll(kk, HEAD_DIM, 1)
    zero = jnp.zeros_like(kk)
    kx_ref[0, 0, 0] = jnp.where(lo, kk, zero).astype(BF16)
    kx_ref[0, 0, 1] = jnp.where(lo, zero, kr).astype(BF16)
    kx_ref[0, 1, 0] = jnp.where(lo, kr, zero).astype(BF16)
    kx_ref[0, 1, 1] = jnp.where(lo, zero, kk).astype(BF16)
    vv = p[:, QK_WIDTH:QK_WIDTH + A_KV_WIDTH]
    vr = pltpu.roll(vv, HEAD_DIM, 1)
    vx_ref[0, 0, 0] = jnp.where(lo, vv, zero).astype(BF16)
    vx_ref[0, 0, 1] = jnp.where(lo, zero, vr).astype(BF16)
    vx_ref[0, 1, 0] = jnp.where(lo, vr, zero).astype(BF16)
    vx_ref[0, 1, 1] = jnp.where(lo, zero, vv).astype(BF16)

    o3 = QK_WIDTH + A_KV_WIDTH
    scale = HEAD_DIM ** -0.5
    for k in range(B_WIDTH // LANES):
        cq = p[:, o3 + k * LANES:o3 + (k + 1) * LANES] * scale
        qlo_ref[0, :, k * LANES:(k + 1) * LANES] = jnp.where(lo, cq, zero).astype(BF16)
        qhi_ref[0, :, k * LANES:(k + 1) * LANES] = jnp.where(lo, zero, cq).astype(BF16)
    kb_ref[0] = p[:, o3 + B_WIDTH:o3 + 2 * B_WIDTH].astype(BF16)
    vb_ref[0] = p[:, o3 + 2 * B_WIDTH:o3 + 3 * B_WIDTH].astype(BF16)


def _inproj(x, shift1, scale1, g1, w_in, bd, gqk, cos_t, sin_t, tm):
    bsz, seq, d = x.shape
    n = w_in.shape[1]
    row = lambda b, i: (b, i, 0)
    ext = lambda b, i: (b, 0, 0, i, 0)
    bcast = lambda b, i: (b, 0, 0)
    const = lambda b, i: (0, 0)
    sd = jax.ShapeDtypeStruct
    return pl.pallas_call(
        _inproj_kernel,
        grid=(bsz, seq // tm),
        in_specs=[pl.BlockSpec((1, tm, d), row),
                  pl.BlockSpec((1, 1, d), bcast),
                  pl.BlockSpec((1, 1, d), bcast),
                  pl.BlockSpec((1, d), const),
                  pl.BlockSpec((d, n), const),
                  pl.BlockSpec((QK_WIDTH, QK_WIDTH), const),
                  pl.BlockSpec((1, QK_WIDTH), const),
                  pl.BlockSpec((tm, LANES), lambda b, i: (i, 0)),
                  pl.BlockSpec((tm, LANES), lambda b, i: (i, 0))],
        out_specs=[pl.BlockSpec((1, tm, A_WIDTH), row),
                   pl.BlockSpec((1, 2, 2, tm, LANES), ext),
                   pl.BlockSpec((1, 2, 2, tm, LANES), ext),
                   pl.BlockSpec((1, tm, B_WIDTH), row),
                   pl.BlockSpec((1, tm, B_WIDTH), row),
                   pl.BlockSpec((1, tm, B_WIDTH), row),
                   pl.BlockSpec((1, tm, B_WIDTH), row)],
        out_shape=[sd((bsz, seq, A_WIDTH), BF16),
                   sd((bsz, 2, 2, seq, LANES), BF16),
                   sd((bsz, 2, 2, seq, LANES), BF16),
                   sd((bsz, seq, B_WIDTH), BF16),
                   sd((bsz, seq, B_WIDTH), BF16),
                   sd((bsz, seq, B_WIDTH), BF16),
                   sd((bsz, seq, B_WIDTH), BF16)],
        compiler_params=_cparams(("arbitrary", "arbitrary")),
        name="inproj",
    )(x, shift1, scale1, g1, w_in, bd, gqk, cos_t, sin_t)


def _gqa_kernel(q_ref, k_ref, v_ref, o_ref, *, tk):
    tq = q_ref.shape[1]
    seq = k_ref.shape[3]
    lane = lax.broadcasted_iota(jnp.int32, (tq, LANES), 1)
    lo = lane < HEAD_DIM
    nt = (((1,), (1,)), ((), ()))
    for j in range(2):
        qc = q_ref[0, :, j * LANES:(j + 1) * LANES]

        def body(c, carry):
            m0, l0, m1, l1, acc = carry
            start = pl.multiple_of(c * tk, tk)
            k0 = k_ref[0, 0, 0, pl.ds(start, tk), :]
            k1 = k_ref[0, 0, 1, pl.ds(start, tk), :]
            s0 = lax.dot_general(qc, k0, nt, preferred_element_type=F32)
            s1 = lax.dot_general(qc, k1, nt, preferred_element_type=F32)
            m0n = jnp.maximum(m0, jnp.max(s0, axis=-1, keepdims=True))
            m1n = jnp.maximum(m1, jnp.max(s1, axis=-1, keepdims=True))
            a0 = jnp.exp(m0 - m0n)
            a1 = jnp.exp(m1 - m1n)
            p0 = jnp.exp(s0 - m0n)
            p1 = jnp.exp(s1 - m1n)
            l0 = a0 * l0 + jnp.sum(p0, axis=-1, keepdims=True)
            l1 = a1 * l1 + jnp.sum(p1, axis=-1, keepdims=True)
            v0 = v_ref[0, 0, 0, pl.ds(start, tk), :]
            v1 = v_ref[0, 0, 1, pl.ds(start, tk), :]
            pv = (jnp.dot(p0.astype(BF16), v0, preferred_element_type=F32)
                  + jnp.dot(p1.astype(BF16), v1, preferred_element_type=F32))
            acc = acc * jnp.where(lo, a0, a1) + pv
            return m0n, l0, m1n, l1, acc

        init = (jnp.full((tq, 1), NEG_BIG, F32), jnp.zeros((tq, 1), F32),
                jnp.full((tq, 1), NEG_BIG, F32), jnp.zeros((tq, 1), F32),
                jnp.zeros((tq, LANES), F32))
        m0, l0, m1, l1, acc = lax.fori_loop(0, seq // tk, body, init)
        o_ref[0, :, j * LANES:(j + 1) * LANES] = (acc / jnp.where(lo, l0, l1)).astype(BF16)


def _gqa(qa, kx, vx, tq, tk):
    bsz, seq, _ = qa.shape
    kv = lambda b, g, i: (b, g, 0, 0, 0)
    return pl.pallas_call(
        functools.partial(_gqa_kernel, tk=tk),
        grid=(bsz, A_KV_HEADS, seq // tq),
        in_specs=[pl.BlockSpec((1, tq, 2 * LANES), lambda b, g, i: (b, i, g)),
                  pl.BlockSpec((1, 1, 2, seq, LANES), kv),
                  pl.BlockSpec((1, 1, 2, seq, LANES), kv)],
        out_specs=pl.BlockSpec((1, tq, 2 * LANES), lambda b, g, i: (b, i, g)),
        out_shape=jax.ShapeDtypeStruct((bsz, seq, A_WIDTH), BF16),
        compiler_params=_cparams(("arbitrary", "arbitrary", "arbitrary")),
        name="gqa",
    )(qa, kx, vx)


def _natten_kernel(qlo_ref, qhi_ref, k0, k1, k2, k3, k4, v0, v1, v2, v3, v4, bias_ref, o_ref):
    ks = jnp.concatenate([k0[0], k1[0], k2[0], k3[0], k4[0]], axis=0)
    vs = jnp.concatenate([v0[0], v1[0], v2[0], v3[0], v4[0]], axis=0)
    lane = lax.broadcasted_iota(jnp.int32, (Q_BLOCK, LANES), 1)
    lo = lane < HEAD_DIM
    nt = (((1,), (1,)), ((), ()))
    for j in range(B_WIDTH // LANES):
        sl = slice(j * LANES, (j + 1) * LANES)
        kc = ks[:, sl]
        vc = vs[:, sl]
        outs = []
        for half, q_ref in enumerate((qlo_ref, qhi_ref)):
            s = lax.dot_general(q_ref[0, :, sl], kc, nt, preferred_element_type=F32)
            s = s + bias_ref[0, 2 * j + half]
            m = jnp.max(s, axis=-1, keepdims=True)
            p = jnp.exp(s - m)
            l = jnp.sum(p, axis=-1, keepdims=True)
            outs.append(jnp.dot(p.astype(BF16), vc, preferred_element_type=F32) / l)
        o_ref[0, :, sl] = jnp.where(lo, outs[0], outs[1]).astype(BF16)


def _natten(qlo, qhi, kb, vb, bias):
    bsz, seq, _ = qlo.shape
    nblk = seq // Q_BLOCK
    qspec = pl.BlockSpec((1, Q_BLOCK, B_WIDTH), lambda b, j: (b, j, 0))

    def slab(i):
        return pl.BlockSpec((1, Q_BLOCK, B_WIDTH),
                            lambda b, j: (b, jnp.clip(j - 2, 0, nblk - NA_SLAB_BLOCKS) + i, 0))

    def btype(b, j):
        t = jnp.where(j < 2, j, jnp.where(j >= nblk - 2, j - (nblk - NA_SLAB_BLOCKS), 2))
        return (t, 0, 0, 0)

    kspecs = [slab(i) for i in range(NA_SLAB_BLOCKS)]
    return pl.pallas_call(
        _natten_kernel,
        grid=(bsz, nblk),
        in_specs=[qspec, qspec] + kspecs + kspecs +
                 [pl.BlockSpec((1, B_HEADS, Q_BLOCK, NA_SLAB_BLOCKS * Q_BLOCK), btype)],
        out_specs=pl.BlockSpec((1, Q_BLOCK, B_WIDTH), lambda b, j: (b, j, 0)),
        out_shape=jax.ShapeDtypeStruct((bsz, seq, B_WIDTH), BF16),
        compiler_params=_cparams(("arbitrary", "arbitrary")),
        name="natten",
    )(qlo, qhi, *([kb] * NA_SLAB_BLOCKS), *([vb] * NA_SLAB_BLOCKS), bias)


def _natten_bias_tables(rpb, seq):
    rows = seq // GRID_W
    nblk = seq // Q_BLOCK
    assert rows >= 2 * NA_SLAB_BLOCKS and NA_KH <= rows
    reps = [0, 1, 2, nblk - 2, nblk - 1]
    dri_l, dci_l, valid_l = [], [], []
    for jr in reps:
        t = jr * Q_BLOCK + np.arange(Q_BLOCK)
        r, c = t // GRID_W, t % GRID_W
        rs = np.clip(r - NA_KH // 2, 0, rows - NA_KH)
        cs = np.clip(c - NA_KW // 2, 0, GRID_W - NA_KW)
        row0 = int(np.clip(jr - 2, 0, nblk - NA_SLAB_BLOCKS)) * (Q_BLOCK // GRID_W)
        s = np.arange(NA_SLAB_BLOCKS * Q_BLOCK)
        key_r, key_c = row0 + s // GRID_W, s % GRID_W
        valid = ((key_r[None, :] >= rs[:, None]) & (key_r[None, :] < rs[:, None] + NA_KH)
                 & (key_c[None, :] >= cs[:, None]) & (key_c[None, :] < cs[:, None] + NA_KW))
        dri_l.append(np.clip(key_r[None, :] - r[:, None] + (NA_KH - 1), 0, 2 * NA_KH - 2))
        dci_l.append(np.clip(key_c[None, :] - c[:, None] + (NA_KW - 1), 0, 2 * NA_KW - 2))
        valid_l.append(valid)
    dri, dci, valid = np.stack(dri_l), np.stack(dci_l), np.stack(valid_l)
    bias = rpb[:, dri, dci]
    bias = jnp.where(valid[None], bias, NEG_BIG)
    return jnp.transpose(bias, (1, 0, 2, 3)).astype(F32)


def _outproj_kernel(oa_ref, ob_ref, x_ref, ga_ref, gb_ref, w_ref, gate_ref, g2_ref, sc_ref, sh_ref,
                    x1_ref, h2_ref):
    def gnorm(o, g):
        o = o.astype(F32)
        return o * lax.rsqrt(jnp.mean(o * o, axis=-1, keepdims=True) + EPS) * g

    na = gnorm(oa_ref[0], ga_ref[...]).astype(BF16)
    nb = gnorm(ob_ref[0], gb_ref[...]).astype(BF16)
    mix = (jnp.dot(na, w_ref[:A_WIDTH, :], preferred_element_type=F32)
           + jnp.dot(nb, w_ref[A_WIDTH:, :], preferred_element_type=F32))
    x1 = x_ref[0] + gate_ref[0] * mix
    x1_ref[0] = x1
    h2 = x1 * lax.rsqrt(jnp.mean(x1 * x1, axis=-1, keepdims=True) + EPS) * g2_ref[...]
    h2_ref[0] = (h2 * (1.0 + sc_ref[0]) + sh_ref[0]).astype(BF16)


def _outproj(oa, ob, x, ga, gb, w_out, gate1, g2, scale2, shift2, tm):
    bsz, seq, d = x.shape
    row = lambda b, i: (b, i, 0)
    bcast = lambda b, i: (b, 0, 0)
    const = lambda b, i: (0, 0)
    return pl.pallas_call(
        _outproj_kernel,
        grid=(bsz, seq // tm),
        in_specs=[pl.BlockSpec((1, tm, A_WIDTH), row),
                  pl.BlockSpec((1, tm, B_WIDTH), row),
                  pl.BlockSpec((1, tm, d), row),
                  pl.BlockSpec((1, A_WIDTH), const),
                  pl.BlockSpec((1, B_WIDTH), const),
                  pl.BlockSpec((A_WIDTH + B_WIDTH, d), const),
                  pl.BlockSpec((1, 1, d), bcast),
                  pl.BlockSpec((1, d), const),
                  pl.BlockSpec((1, 1, d), bcast),
                  pl.BlockSpec((1, 1, d), bcast)],
        out_specs=[pl.BlockSpec((1, tm, d), row), pl.BlockSpec((1, tm, d), row)],
        out_shape=[jax.ShapeDtypeStruct((bsz, seq, d), F32),
                   jax.ShapeDtypeStruct((bsz, seq, d), BF16)],
        compiler_params=_cparams(("arbitrary", "arbitrary")),
        name="outproj",
    )(oa, ob, x, ga, gb, w_out, gate1, g2, scale2, shift2)


def _top16_axis0(s, payload=None):
    n = s.shape[0]
    iota = lax.broadcasted_iota(jnp.int32, s.shape, 0)
    vals, picks = [], []
    for _ in range(PEER_TOPK):
        m = jnp.max(s, axis=0, keepdims=True)
        first = jnp.min(jnp.where(s == m, iota, n), axis=0, keepdims=True)
        hit = iota == first
        vals.append(m)
        if payload is None:
            picks.append(first)
        else:
            picks.append(jnp.sum(jnp.where(hit, payload, 0), axis=0, keepdims=True))
        s = jnp.where(hit, -jnp.inf, s)
    return jnp.concatenate(vals, axis=0), jnp.concatenate(picks, axis=0)


def _peer_topk_kernel(h_ref, wq_ref, sk1_ref, sk2_ref, idx_ref, gate_ref):
    q = jnp.dot(h_ref[...], wq_ref[...], preferred_element_type=F32)
    nt = (((1,), (1,)), ((), ()))
    for hh in range(PEER_HEADS):
        base = 2 * hh * PEER_SUB_DIM
        q1 = q[:, base:base + PEER_SUB_DIM].astype(BF16)
        q2 = q[:, base + PEER_SUB_DIM:base + 2 * PEER_SUB_DIM].astype(BF16)
        s1 = lax.dot_general(sk1_ref[...], q1, nt, preferred_element_type=F32)
        s2 = lax.dot_general(sk2_ref[...], q2, nt, preferred_element_type=F32)
        v1, i1 = _top16_axis0(s1)
        v2, i2 = _top16_axis0(s2)
        cand = jnp.concatenate([v1[a:a + 1] + v2 for a in range(PEER_TOPK)], axis=0)
        cidx = jnp.concatenate([i1[a:a + 1] * PEER_N_KEYS + i2 for a in range(PEER_TOPK)], axis=0)
        top, eidx = _top16_axis0(cand, payload=cidx)
        e = jnp.exp(top - top[0:1])
        gate = e / jnp.sum(e, axis=0, keepdims=True)
        idx_ref[hh * PEER_TOPK:(hh + 1) * PEER_TOPK, :] = eidx
        gate_ref[hh * PEER_TOPK:(hh + 1) * PEER_TOPK, :] = gate


def _peer_topk(h2, wq, sk1, sk2, tt):
    n, d = h2.shape
    const = lambda i: (0, 0)
    return pl.pallas_call(
        _peer_topk_kernel,
        grid=(n // tt,),
        in_specs=[pl.BlockSpec((tt, d), lambda i: (i, 0)),
                  pl.BlockSpec(wq.shape, const),
                  pl.BlockSpec(sk1.shape, const),
                  pl.BlockSpec(sk2.shape, const)],
        out_specs=[pl.BlockSpec((PEER_SLOTS, tt), lambda i: (0, i)),
                   pl.BlockSpec((PEER_SLOTS, tt), lambda i: (0, i))],
        out_shape=[jax.ShapeDtypeStruct((PEER_SLOTS, n), jnp.int32),
                   jax.ShapeDtypeStruct((PEER_SLOTS, n), F32)],
        compiler_params=_cparams(("arbitrary",)),
        name="peer_topk",
    )(h2, wq, sk1, sk2)


PEER_TOK_BLOCK = 128
PEER_TOK_GROUP = 8
PEER_GROUPS = PEER_TOK_BLOCK // PEER_TOK_GROUP
PEER_SLOT_ROWS = PEER_TOK_GROUP * PEER_SLOTS


def _peer_ffn_kernel(idx_ref, gcol_ref, h_ref, x1_ref, gate2_ref, gf_ref, uv_ref, o_ref,
                     buf_ref, sem_ref):
    d = h_ref.shape[1]

    def issue(grp, slot):
        def tok(t, carry):
            def row(e, carry2):
                r = idx_ref[grp * PEER_TOK_GROUP + t, e]
                pltpu.make_async_copy(uv_ref.at[pl.ds(r, 1), :],
                                      buf_ref.at[slot, pl.ds(t * PEER_SLOTS + e, 1), :],
                                      sem_ref.at[slot]).start()
                return carry2
            return lax.fori_loop(0, PEER_SLOTS, row, carry, unroll=8)
        lax.fori_loop(0, PEER_TOK_GROUP, tok, 0)

    def wait(slot):
        pltpu.make_async_copy(uv_ref.at[pl.ds(0, PEER_SLOT_ROWS), :], buf_ref.at[slot],
                              sem_ref.at[slot]).wait()

    issue(0, 0)

    def group(grp, carry):
        slot = grp % 2

        @pl.when(grp + 1 < PEER_GROUPS)
        def _():
            issue(grp + 1, 1 - slot)

        wait(slot)
        gcols = gcol_ref[grp]
        t0 = pl.multiple_of(grp * PEER_TOK_GROUP, PEER_TOK_GROUP)
        hs = h_ref[pl.ds(t0, PEER_TOK_GROUP), :].astype(F32)
        outs = []
        for t in range(PEER_TOK_GROUP):
            w = buf_ref[slot, t * PEER_SLOTS:(t + 1) * PEER_SLOTS, :]
            u = pltpu.bitcast(w & jnp.uint32(0xFFFF0000), F32)
            v = pltpu.bitcast(w << 16, F32)
            a = jnp.sum(u * hs[t:t + 1, :], axis=-1, keepdims=True)
            act = 0.5 * a * (1.0 + lax.erf(a * (2.0 ** -0.5)))
            wgt = act * gcols[:, t:t + 1]
            outs.append(jnp.sum(v * wgt, axis=0, keepdims=True))
        ffn = jnp.concatenate(outs, axis=0)
        y = x1_ref[pl.ds(t0, PEER_TOK_GROUP), :] + gate2_ref[0] * ffn
        y = y * lax.rsqrt(jnp.mean(y * y, axis=-1, keepdims=True) + EPS) * gf_ref[...]
        o_ref[pl.ds(t0, PEER_TOK_GROUP), :] = y
        return carry

    lax.fori_loop(0, PEER_GROUPS, group, 0)


def _peer_ffn(idx, gcol, h2, x1, gate2, gf, uv, seq):
    n, d = h2.shape
    blocks_per_batch = seq // PEER_TOK_BLOCK
    row = lambda i: (i, 0)
    return pl.pallas_call(
        _peer_ffn_kernel,
        grid=(n // PEER_TOK_BLOCK,),
        in_specs=[pl.BlockSpec((PEER_TOK_BLOCK, PEER_SLOTS), row, memory_space=pltpu.SMEM),
                  pl.BlockSpec((PEER_GROUPS, PEER_SLOTS, PEER_TOK_GROUP), lambda i: (i, 0, 0)),
                  pl.BlockSpec((PEER_TOK_BLOCK, d), row),
                  pl.BlockSpec((PEER_TOK_BLOCK, d), row),
                  pl.BlockSpec((1, 1, d), lambda i: (i // blocks_per_batch, 0, 0)),
                  pl.BlockSpec((1, d), lambda i: (0, 0)),
                  pl.BlockSpec(memory_space=pl.ANY)],
        out_specs=pl.BlockSpec((PEER_TOK_BLOCK, d), row),
        out_shape=jax.ShapeDtypeStruct((n, d), F32),
        scratch_shapes=[pltpu.VMEM((2, PEER_SLOT_ROWS, d), jnp.uint32),
                        pltpu.SemaphoreType.DMA((2,))],
        compiler_params=_cparams(("arbitrary",)),
        name="peer_ffn",
    )(idx, gcol, h2, x1, gate2, gf, uv)


def _rope_tables(seq):
    t = np.arange(seq)
    pos = np.stack([t // GRID_W, t % GRID_W], axis=1).astype(np.float32)
    half = HEAD_DIM // 4
    dlane = np.arange(HEAD_DIM)
    axis = dlane // (HEAD_DIM // 2)
    fidx = dlane % half
    sign = np.where((dlane % (HEAD_DIM // 2)) < half, -1.0, 1.0).astype(np.float32)
    freqs = ROPE_THETA ** (-jnp.arange(half, dtype=F32) / half)
    ang = jnp.asarray(pos)[:, axis] * freqs[fidx][None, :]
    cos = jnp.cos(ang)
    sin = jnp.sin(ang) * sign[None, :]
    return jnp.tile(cos, (1, 2)), jnp.tile(sin, (1, 2))


def _pack_expert_tables(peer_u, peer_v):
    ub = lax.bitcast_convert_type(peer_u.astype(BF16), jnp.uint16).astype(jnp.uint32)
    vb = lax.bitcast_convert_type(peer_v.astype(BF16), jnp.uint16).astype(jnp.uint32)
    return (ub << 16) | vb


def kernel(x, c, w_ada, b_ada, norm1_g, w_in, q_norm_g, k_norm_g, natten_rpb, group_norm_a_g,
           group_norm_b_g, w_out, norm2_g, peer_w_query, peer_sub_keys_1, peer_sub_keys_2, peer_u,
           peer_v, final_norm_g):
    bsz, seq, d = x.shape
    depth = w_ada.shape[0]
    n = bsz * seq
    tm = min(512, seq)
    assert seq % tm == 0 and seq % PEER_TOK_BLOCK == 0 and seq % GRID_W == 0

    cos_t, sin_t = _rope_tables(seq)
    bd = jnp.asarray((np.arange(QK_WIDTH)[:, None] // HEAD_DIM
                      == np.arange(QK_WIDTH)[None, :] // HEAD_DIM).astype(np.float32))
    c8 = jnp.pad(c, ((0, 8 - bsz % 8 if bsz % 8 else 0), (0, 0)))

    for l in range(depth):
        mod = _ada(c8, w_ada[l], b_ada[l])[:bsz]
        shift1, scale1, gate1, shift2, scale2, gate2 = [m.reshape(bsz, 1, d) for m in jnp.split(mod, 6, axis=-1)]
        gqk = jnp.concatenate([jnp.tile(q_norm_g[l] * (HEAD_DIM ** -0.5), A_HEADS),
                               jnp.tile(k_norm_g[l], A_KV_HEADS)]).reshape(1, QK_WIDTH)
        qa, kx, vx, qlo, qhi, kb, vb = _inproj(
            x, shift1, scale1, norm1_g[l].reshape(1, d), w_in[l].astype(BF16), bd, gqk, cos_t, sin_t, tm)
        out_a = _gqa(qa, kx, vx, tq=min(256, seq), tk=min(512, seq))
        out_b = _natten(qlo, qhi, kb, vb, _natten_bias_tables(natten_rpb[l], seq))
        x1, h2 = _outproj(out_a, out_b, x, group_norm_a_g[l].reshape(1, A_WIDTH),
                          group_norm_b_g[l].reshape(1, B_WIDTH), w_out[l].astype(BF16), gate1,
                          norm2_g[l].reshape(1, d), scale2, shift2, tm)
        h2f = h2.reshape(n, d)
        idx_t, gate_t = _peer_topk(h2f, peer_w_query[l].astype(BF16), peer_sub_keys_1[l].astype(BF16),
                                   peer_sub_keys_2[l].astype(BF16), tt=min(256, n))
        idx = idx_t.T
        gcol = gate_t.reshape(PEER_SLOTS, n // PEER_TOK_GROUP, PEER_TOK_GROUP).transpose(1, 0, 2)
        uv = _pack_expert_tables(peer_u[l], peer_v[l])
        last = l == depth - 1
        assert last, "final norm is fused into the PEER stage; depth > 1 needs an un-normed variant"
        x = _peer_ffn(idx, gcol, h2f, x1.reshape(n, d), gate2, final_norm_g.reshape(1, d), uv,
                      seq).reshape(bsz, seq, d)
    return x
```

```python
import functools

import numpy as np
import jax
import jax.numpy as jnp
from jax import lax
from jax.experimental import pallas as pl
from jax.experimental.pallas import tpu as pltpu

F32 = jnp.float32
BF16 = jnp.bfloat16

GRID_W = 64
HEAD_DIM = 64
A_HEADS = 8
A_KV_HEADS = 2
B_HEADS = 8
A_WIDTH = A_HEADS * HEAD_DIM
A_KV_WIDTH = A_KV_HEADS * HEAD_DIM
B_WIDTH = B_HEADS * HEAD_DIM
QK_WIDTH = A_WIDTH + A_KV_WIDTH
ROPE_THETA = 10000.0
NA_KH = 8
NA_KW = 16
PEER_HEADS = 8
PEER_N_KEYS = 128
PEER_SUB_DIM = 128
PEER_TOPK = 16
PEER_SLOTS = PEER_HEADS * PEER_TOPK
EPS = 1e-6

LANES = 128
Q_BLOCK = 128
NA_SLAB_BLOCKS = 5
NEG_BIG = -1e30
VMEM_LIMIT = 48 * 1024 * 1024


def _cparams(sem):
    return pltpu.CompilerParams(dimension_semantics=sem, vmem_limit_bytes=VMEM_LIMIT)


def _ada_kernel(c_ref, w_ref, b_ref, o_ref):
    c = c_ref[...]
    sc = c / (1.0 + jnp.exp(-c))
    o_ref[...] = jnp.dot(sc, w_ref[...], preferred_element_type=F32) + b_ref[...]


def _ada(c8, w_ada, b_ada):
    rows, d = c8.shape
    n = w_ada.shape[1]
    tn = 512
    return pl.pallas_call(
        _ada_kernel,
        grid=(n // tn,),
        in_specs=[pl.BlockSpec((rows, d), lambda j: (0, 0)),
                  pl.BlockSpec((d, tn), lambda j: (0, j)),
                  pl.BlockSpec((1, tn), lambda j: (0, j))],
        out_specs=pl.BlockSpec((rows, tn), lambda j: (0, j)),
        out_shape=jax.ShapeDtypeStruct((rows, n), F32),
        compiler_params=_cparams(("arbitrary",)),
        name="ada",
    )(c8, w_ada, b_ada.reshape(1, n))


def _inproj_kernel(x_ref, sh_ref, sc_ref, g1_ref, w_ref, bd_ref, gqk_ref, cos_ref, sin_ref,
                   qa_ref, kx_ref, vx_ref, qlo_ref, qhi_ref, kb_ref, vb_ref):
    x = x_ref[0]
    ms = jnp.mean(x * x, axis=-1, keepdims=True)
    h = x * lax.rsqrt(ms + EPS) * g1_ref[...]
    h = h * (1.0 + sc_ref[0]) + sh_ref[0]
    p = jnp.dot(h.astype(BF16), w_ref[...], preferred_element_type=F32)
    tm = p.shape[0]

    lane = lax.broadcasted_iota(jnp.int32, (tm, LANES), 1)
    lo = lane < HEAD_DIM
    first16 = (lane % 32) < 16

    pq = p[:, :QK_WIDTH]
    ss = jnp.dot(pq * pq, bd_ref[...], preferred_element_type=F32)
    qn = pq * lax.rsqrt(ss * (1.0 / HEAD_DIM) + EPS) * gqk_ref[...]
    cos = cos_ref[...]
    sin = sin_ref[...]

    def rope(cn):
        partner = jnp.where(first16, pltpu.roll(cn, LANES - 16, 1), pltpu.roll(cn, 16, 1))
        return cn * cos + partner * sin

    for k in range(A_WIDTH // LANES):
        qa_ref[0, :, k * LANES:(k + 1) * LANES] = rope(qn[:, k * LANES:(k + 1) * LANES]).astype(BF16)
    kk = rope(qn[:, A_WIDTH:QK_WIDTH])
    kr = pltpu.roll(kk, HEAD_DIM, 1)
    zero = jnp.zeros_like(kk)
    kx_ref[0, 0, 0] = jnp.where(lo, kk, zero).astype(BF16)
    kx_ref[0, 0, 1] = jnp.where(lo, zero, kr).astype(BF16)
    kx_ref[0, 1, 0] = jnp.where(lo, kr, zero).astype(BF16)
    kx_ref[0, 1, 1] = jnp.where(lo, zero, kk).astype(BF16)
    vv = p[:, QK_WIDTH:QK_WIDTH + A_KV_WIDTH]
    vr = pltpu.roll(vv, HEAD_DIM, 1)
    vx_ref[0, 0, 0] = jnp.where(lo, vv, zero).astype(BF16)
    vx_ref[0, 0, 1] = jnp.where(lo, zero, vr).astype(BF16)
    vx_ref[0, 1, 0] = jnp.where(lo, vr, zero).astype(BF16)
    vx_ref[0, 1, 1] = jnp.where(lo, zero, vv).astype(BF16)

    o3 = QK_WIDTH + A_KV_WIDTH
    scale = HEAD_DIM ** -0.5
    for k in range(B_WIDTH // LANES):
        cq = p[:, o3 + k * LANES:o3 + (k + 1) * LANES] * scale
        qlo_ref[0, :, k * LANES:(k + 1) * LANES] = jnp.where(lo, cq, zero).astype(BF16)
        qhi_ref[0, :, k * LANES:(k + 1) * LANES] = jnp.where(lo, zero, cq).astype(BF16)
    kb_ref[0] = p[:, o3 + B_WIDTH:o3 + 2 * B_WIDTH].astype(BF16)
    vb_ref[0] = p[:, o3 + 2 * B_WIDTH:o3 + 3 * B_WIDTH].astype(BF16)


def _inproj(x, shift1, scale1, g1, w_in, bd, gqk, cos_t, sin_t, tm):
    bsz, seq, d = x.shape
    n = w_in.shape[1]
    row = lambda b, i: (b, i, 0)
    ext = lambda b, i: (b, 0, 0, i, 0)
    bcast = lambda b, i: (b, 0, 0)
    const = lambda b, i: (0, 0)
    sd = jax.ShapeDtypeStruct
    return pl.pallas_call(
        _inproj_kernel,
        grid=(bsz, seq // tm),
        in_specs=[pl.BlockSpec((1, tm, d), row),
                  pl.BlockSpec((1, 1, d), bcast),
                  pl.BlockSpec((1, 1, d), bcast),
                  pl.BlockSpec((1, d), const),
                  pl.BlockSpec((d, n), const),
                  pl.BlockSpec((QK_WIDTH, QK_WIDTH), const),
                  pl.BlockSpec((1, QK_WIDTH), const),
                  pl.BlockSpec((tm, LANES), lambda b, i: (i, 0)),
                  pl.BlockSpec((tm, LANES), lambda b, i: (i, 0))],
        out_specs=[pl.BlockSpec((1, tm, A_WIDTH), row),
                   pl.BlockSpec((1, 2, 2, tm, LANES), ext),
                   pl.BlockSpec((1, 2, 2, tm, LANES), ext),
                   pl.BlockSpec((1, tm, B_WIDTH), row),
                   pl.BlockSpec((1, tm, B_WIDTH), row),
                   pl.BlockSpec((1, tm, B_WIDTH), row),
                   pl.BlockSpec((1, tm, B_WIDTH), row)],
        out_shape=[sd((bsz, seq, A_WIDTH), BF16),
                   sd((bsz, 2, 2, seq, LANES), BF16),
                   sd((bsz, 2, 2, seq, LANES), BF16),
                   sd((bsz, seq, B_WIDTH), BF16),
                   sd((bsz, seq, B_WIDTH), BF16),
                   sd((bsz, seq, B_WIDTH), BF16),
                   sd((bsz, seq, B_WIDTH), BF16)],
        compiler_params=_cparams(("arbitrary", "arbitrary")),
        name="inproj",
    )(x, shift1, scale1, g1, w_in, bd, gqk, cos_t, sin_t)


def _gqa_kernel(q_ref, k_ref, v_ref, o_ref, *, tk):
    tq = q_ref.shape[1]
    seq = k_ref.shape[3]
    rows = 2 * tq
    lane = lax.broadcasted_iota(jnp.int32, (rows, LANES), 1)
    lo = lane < HEAD_DIM
    nt = (((1,), (1,)), ((), ()))
    qc = jnp.concatenate([q_ref[0, :, :LANES], q_ref[0, :, LANES:]], axis=0)

    def body(c, carry):
        m0, l0, m1, l1, acc = carry
        start = pl.multiple_of(c * tk, tk)
        k0 = k_ref[0, 0, 0, pl.ds(start, tk), :]
        k1 = k_ref[0, 0, 1, pl.ds(start, tk), :]
        s0 = lax.dot_general(qc, k0, nt, preferred_element_type=F32)
        s1 = lax.dot_general(qc, k1, nt, preferred_element_type=F32)
        m0n = jnp.maximum(m0, jnp.max(s0, axis=-1, keepdims=True))
        m1n = jnp.maximum(m1, jnp.max(s1, axis=-1, keepdims=True))
        a0 = jnp.exp(m0 - m0n)
        a1 = jnp.exp(m1 - m1n)
        p0 = jnp.exp(s0 - m0n)
        p1 = jnp.exp(s1 - m1n)
        l0 = a0 * l0 + jnp.sum(p0, axis=-1, keepdims=True)
        l1 = a1 * l1 + jnp.sum(p1, axis=-1, keepdims=True)
        v0 = v_ref[0, 0, 0, pl.ds(start, tk), :]
        v1 = v_ref[0, 0, 1, pl.ds(start, tk), :]
        pv = (jnp.dot(p0.astype(BF16), v0, preferred_element_type=F32)
              + jnp.dot(p1.astype(BF16), v1, preferred_element_type=F32))
        acc = acc * jnp.where(lo, a0, a1) + pv
        return m0n, l0, m1n, l1, acc

    init = (jnp.full((rows, 1), NEG_BIG, F32), jnp.zeros((rows, 1), F32),
            jnp.full((rows, 1), NEG_BIG, F32), jnp.zeros((rows, 1), F32),
            jnp.zeros((rows, LANES), F32))
    m0, l0, m1, l1, acc = lax.fori_loop(0, seq // tk, body, init)
    out = (acc / jnp.where(lo, l0, l1)).astype(BF16)
    o_ref[0, :, :LANES] = out[:tq]
    o_ref[0, :, LANES:] = out[tq:]


def _gqa(qa, kx, vx, tq, tk):
    bsz, seq, _ = qa.shape
    kv = lambda b, g, i: (b, g, 0, 0, 0)
    return pl.pallas_call(
        functools.partial(_gqa_kernel, tk=tk),
        grid=(bsz, A_KV_HEADS, seq // tq),
        in_specs=[pl.BlockSpec((1, tq, 2 * LANES), lambda b, g, i: (b, i, g)),
                  pl.BlockSpec((1, 1, 2, seq, LANES), kv),
                  pl.BlockSpec((1, 1, 2, seq, LANES), kv)],
        out_specs=pl.BlockSpec((1, tq, 2 * LANES), lambda b, g, i: (b, i, g)),
        out_shape=jax.ShapeDtypeStruct((bsz, seq, A_WIDTH), BF16),
        compiler_params=_cparams(("arbitrary", "arbitrary", "arbitrary")),
        name="gqa",
    )(qa, kx, vx)


def _natten_kernel(qlo_ref, qhi_ref, k0, k1, k2, k3, k4, v0, v1, v2, v3, v4, bias_ref, o_ref):
    ks = jnp.concatenate([k0[0], k1[0], k2[0], k3[0], k4[0]], axis=0)
    vs = jnp.concatenate([v0[0], v1[0], v2[0], v3[0], v4[0]], axis=0)
    lane = lax.broadcasted_iota(jnp.int32, (Q_BLOCK, LANES), 1)
    lo = lane < HEAD_DIM
    nt = (((1,), (1,)), ((), ()))
    for j in range(B_WIDTH // LANES):
        sl = slice(j * LANES, (j + 1) * LANES)
        kc = ks[:, sl]
        vc = vs[:, sl]
        outs = []
        for half, q_ref in enumerate((qlo_ref, qhi_ref)):
            s = lax.dot_general(q_ref[0, :, sl], kc, nt, preferred_element_type=F32)
            s = s + bias_ref[0, 2 * j + half]
            m = jnp.max(s, axis=-1, keepdims=True)
            p = jnp.exp(s - m)
            l = jnp.sum(p, axis=-1, keepdims=True)
            outs.append(jnp.dot(p.astype(BF16), vc, preferred_element_type=F32) / l)
        o_ref[0, :, sl] = jnp.where(lo, outs[0], outs[1]).astype(BF16)


def _natten(qlo, qhi, kb, vb, bias):
    bsz, seq, _ = qlo.shape
    nblk = seq // Q_BLOCK
    qspec = pl.BlockSpec((1, Q_BLOCK, B_WIDTH), lambda b, j: (b, j, 0))

    def slab(i):
        return pl.BlockSpec((1, Q_BLOCK, B_WIDTH),
                            lambda b, j: (b, jnp.clip(j - 2, 0, nblk - NA_SLAB_BLOCKS) + i, 0))

    def btype(b, j):
        t = jnp.where(j < 2, j, jnp.where(j >= nblk - 2, j - (nblk - NA_SLAB_BLOCKS), 2))
        return (t, 0, 0, 0)

    kspecs = [slab(i) for i in range(NA_SLAB_BLOCKS)]
    return pl.pallas_call(
        _natten_kernel,
        grid=(bsz, nblk),
        in_specs=[qspec, qspec] + kspecs + kspecs +
                 [pl.BlockSpec((1, B_HEADS, Q_BLOCK, NA_SLAB_BLOCKS * Q_BLOCK), btype)],
        out_specs=pl.BlockSpec((1, Q_BLOCK, B_WIDTH), lambda b, j: (b, j, 0)),
        out_shape=jax.ShapeDtypeStruct((bsz, seq, B_WIDTH), BF16),
        compiler_params=_cparams(("arbitrary", "arbitrary")),
        name="natten",
    )(qlo, qhi, *([kb] * NA_SLAB_BLOCKS), *([vb] * NA_SLAB_BLOCKS), bias)


def _natten_bias_tables(rpb, seq):
    rows = seq // GRID_W
    nblk = seq // Q_BLOCK
    assert rows >= 2 * NA_SLAB_BLOCKS and NA_KH <= rows
    reps = np.array([0, 1, 2, nblk - 2, nblk - 1])
    qrows = Q_BLOCK // GRID_W
    srows = NA_SLAB_BLOCKS * qrows
    r = reps[:, None] * qrows + np.arange(qrows)[None, :]
    rs = np.clip(r - NA_KH // 2, 0, rows - NA_KH)
    key_r = (np.clip(reps - 2, 0, nblk - NA_SLAB_BLOCKS) * qrows)[:, None] + np.arange(srows)[None, :]
    valid_r = (key_r[:, None, :] >= rs[:, :, None]) & (key_r[:, None, :] < rs[:, :, None] + NA_KH)
    dri = np.clip(key_r[:, None, :] - r[:, :, None] + (NA_KH - 1), 0, 2 * NA_KH - 2)
    c = np.arange(GRID_W)
    cs = np.clip(c - NA_KW // 2, 0, GRID_W - NA_KW)
    valid_c = (c[None, :] >= cs[:, None]) & (c[None, :] < cs[:, None] + NA_KW)
    dci = np.clip(c[None, :] - c[:, None] + (NA_KW - 1), 0, 2 * NA_KW - 2)
    oh_r = jnp.asarray((dri[..., None] == np.arange(2 * NA_KH - 1)).astype(np.float32))
    oh_c = jnp.asarray((dci[..., None] == np.arange(2 * NA_KW - 1)).astype(np.float32))
    hp = lax.Precision.HIGHEST
    t1 = jnp.einsum('zrka,hab->zhrkb', oh_r, rpb, precision=hp)
    bias = jnp.einsum('zhrkb,qcb->zhrqkc', t1, oh_c, precision=hp)
    valid = valid_r[:, None, :, None, :, None] & valid_c[None, None, None, :, None, :]
    bias = jnp.where(jnp.asarray(valid), bias, NEG_BIG)
    return bias.reshape(len(reps), rpb.shape[0], Q_BLOCK, NA_SLAB_BLOCKS * Q_BLOCK).astype(F32)


def _outproj_kernel(oa_ref, ob_ref, x_ref, ga_ref, gb_ref, w_ref, gate_ref, g2_ref, sc_ref, sh_ref,
                    x1_ref, h2_ref):
    def gnorm(o, g):
        o = o.astype(F32)
        return o * lax.rsqrt(jnp.mean(o * o, axis=-1, keepdims=True) + EPS) * g

    na = gnorm(oa_ref[0], ga_ref[...]).astype(BF16)
    nb = gnorm(ob_ref[0], gb_ref[...]).astype(BF16)
    mix = (jnp.dot(na, w_ref[:A_WIDTH, :], preferred_element_type=F32)
           + jnp.dot(nb, w_ref[A_WIDTH:, :], preferred_element_type=F32))
    x1 = x_ref[0] + gate_ref[0] * mix
    x1_ref[0] = x1
    h2 = x1 * lax.rsqrt(jnp.mean(x1 * x1, axis=-1, keepdims=True) + EPS) * g2_ref[...]
    h2_ref[0] = (h2 * (1.0 + sc_ref[0]) + sh_ref[0]).astype(BF16)


def _outproj(oa, ob, x, ga, gb, w_out, gate1, g2, scale2, shift2, tm):
    bsz, seq, d = x.shape
    row = lambda b, i: (b, i, 0)
    bcast = lambda b, i: (b, 0, 0)
    const = lambda b, i: (0, 0)
    return pl.pallas_call(
        _outproj_kernel,
        grid=(bsz, seq // tm),
        in_specs=[pl.BlockSpec((1, tm, A_WIDTH), row),
                  pl.BlockSpec((1, tm, B_WIDTH), row),
                  pl.BlockSpec((1, tm, d), row),
                  pl.BlockSpec((1, A_WIDTH), const),
                  pl.BlockSpec((1, B_WIDTH), const),
                  pl.BlockSpec((A_WIDTH + B_WIDTH, d), const),
                  pl.BlockSpec((1, 1, d), bcast),
                  pl.BlockSpec((1, d), const),
                  pl.BlockSpec((1, 1, d), bcast),
                  pl.BlockSpec((1, 1, d), bcast)],
        out_specs=[pl.BlockSpec((1, tm, d), row), pl.BlockSpec((1, tm, d), row)],
        out_shape=[jax.ShapeDtypeStruct((bsz, seq, d), F32),
                   jax.ShapeDtypeStruct((bsz, seq, d), BF16)],
        compiler_params=_cparams(("arbitrary", "arbitrary")),
        name="outproj",
    )(oa, ob, x, ga, gb, w_out, gate1, g2, scale2, shift2)


def _top16_axis0(s, payload=None):
    n = s.shape[0]
    iota = lax.broadcasted_iota(jnp.int32, s.shape, 0)
    vals, picks = [], []
    for _ in range(PEER_TOPK):
        m = jnp.max(s, axis=0, keepdims=True)
        first = jnp.min(jnp.where(s == m, iota, n), axis=0, keepdims=True)
        hit = iota == first
        vals.append(m)
        if payload is None:
            picks.append(first)
        else:
            picks.append(jnp.sum(jnp.where(hit, payload, 0), axis=0, keepdims=True))
        s = jnp.where(hit, -jnp.inf, s)
    return jnp.concatenate(vals, axis=0), jnp.concatenate(picks, axis=0)


def _peer_topk_kernel(h_ref, wq_ref, sk1_ref, sk2_ref, idx_ref, gate_ref):
    q = jnp.dot(h_ref[...], wq_ref[...], preferred_element_type=F32)
    nt = (((1,), (1,)), ((), ()))
    for hh in range(PEER_HEADS):
        base = 2 * hh * PEER_SUB_DIM
        q1 = q[:, base:base + PEER_SUB_DIM].astype(BF16)
        q2 = q[:, base + PEER_SUB_DIM:base + 2 * PEER_SUB_DIM].astype(BF16)
        s1 = lax.dot_general(sk1_ref[...], q1, nt, preferred_element_type=F32)
        s2 = lax.dot_general(sk2_ref[...], q2, nt, preferred_element_type=F32)
        v1, i1 = _top16_axis0(s1)
        v2, i2 = _top16_axis0(s2)
        cand = jnp.concatenate([v1[a:a + 1] + v2 for a in range(PEER_TOPK)], axis=0)
        cidx = jnp.concatenate([i1[a:a + 1] * PEER_N_KEYS + i2 for a in range(PEER_TOPK)], axis=0)
        top, eidx = _top16_axis0(cand, payload=cidx)
        e = jnp.exp(top - top[0:1])
        gate = e / jnp.sum(e, axis=0, keepdims=True)
        idx_ref[hh * PEER_TOPK:(hh + 1) * PEER_TOPK, :] = eidx
        gate_ref[hh * PEER_TOPK:(hh + 1) * PEER_TOPK, :] = gate


def _peer_topk(h2, wq, sk1, sk2, tt):
    n, d = h2.shape
    const = lambda i: (0, 0)
    return pl.pallas_call(
        _peer_topk_kernel,
        grid=(n // tt,),
        in_specs=[pl.BlockSpec((tt, d), lambda i: (i, 0)),
                  pl.BlockSpec(wq.shape, const),
                  pl.BlockSpec(sk1.shape, const),
                  pl.BlockSpec(sk2.shape, const)],
        out_specs=[pl.BlockSpec((PEER_SLOTS, tt), lambda i: (0, i)),
                   pl.BlockSpec((PEER_SLOTS, tt), lambda i: (0, i))],
        out_shape=[jax.ShapeDtypeStruct((PEER_SLOTS, n), jnp.int32),
                   jax.ShapeDtypeStruct((PEER_SLOTS, n), F32)],
        compiler_params=_cparams(("arbitrary",)),
        name="peer_topk",
    )(h2, wq, sk1, sk2)


PEER_TOK_BLOCK = 256
PEER_TOK_GROUP = 8
PEER_GROUPS = PEER_TOK_BLOCK // PEER_TOK_GROUP
PEER_SLOT_ROWS = PEER_TOK_GROUP * PEER_SLOTS
SUBLANES = 8
TILE_ROWS = 2 * SUBLANES
PEER_UNROLL = 4
PEER_CHUNK = 16
PEER_NCHUNK = PEER_SLOTS // PEER_CHUNK


def _peer_ffn_kernel(idx_ref, gcol_ref, h_ref, x1_ref, gate2_ref, gf_ref, gmat_ref, uv_ref,
                     o_ref, buf_ref, sem_ref, wsp_ref):
    step = pl.program_id(0)
    last_step = pl.num_programs(0) - 1
    d = SUBLANES * LANES
    lane = lax.broadcasted_iota(jnp.int32, (PEER_SLOTS, LANES), 1)

    def start_row(row, slot, t, e):
        r = idx_ref[row, e]
        pltpu.make_async_copy(uv_ref.at[r], buf_ref.at[slot, t * PEER_SLOTS + e],
                              sem_ref.at[slot]).start(priority=e % 2)

    def wait(slot):
        pltpu.make_async_copy(uv_ref.at[pl.ds(0, PEER_SLOT_ROWS)], buf_ref.at[slot],
                              sem_ref.at[slot]).wait()

    @pl.when(step == 0)
    def _():
        def tok(t, carry):
            for e in range(PEER_SLOTS):
                start_row(t, 0, t, e)
            return carry
        lax.fori_loop(0, PEER_TOK_GROUP, tok, 0)

    def process(g, carry):
        slot = g % 2
        wait(slot)
        tok0 = pl.multiple_of(g * PEER_TOK_GROUP, PEER_TOK_GROUP)
        next_row0 = tok0 + PEER_TOK_GROUP
        half = PEER_SLOTS // 2

        def tile_f32(t, e):
            return buf_ref[slot, t * PEER_SLOTS + e].astype(F32)

        def u_phase(hg, a):
            for s in range(PEER_UNROLL):
                t = hg * PEER_UNROLL + s
                hv = h_ref[tok0 + t]
                part = jnp.zeros((PEER_SLOTS, LANES), F32)
                for c in range(PEER_NCHUNK):
                    prods = []
                    for k in range(PEER_CHUNK):
                        e = c * PEER_CHUNK + k
                        if k % 2 == 0:
                            start_row(next_row0 + t, 1 - slot, t, e // 2)
                        prods.append(tile_f32(t, e)[:SUBLANES] * hv)
                    prod = jnp.concatenate(prods, axis=0).astype(BF16)
                    part = part + jnp.dot(gmat_ref[:, c * LANES:(c + 1) * LANES], prod,
                                          preferred_element_type=F32)
                a = jnp.where(lane == t, jnp.sum(part, axis=1, keepdims=True), a)
            return a

        a = lax.fori_loop(0, PEER_TOK_GROUP // PEER_UNROLL, u_phase, jnp.zeros((PEER_SLOTS, LANES), F32))
        act = 0.5 * a * (1.0 + lax.erf(a * (2.0 ** -0.5)))
        w8 = act[:, :PEER_TOK_GROUP] * gcol_ref[g]
        for t in range(PEER_TOK_GROUP):
            wsp_ref[t] = jnp.broadcast_to(w8[:, t:t + 1], (PEER_SLOTS, LANES))

        def v_phase(hg, c2):
            ys = []
            for s in range(PEER_UNROLL):
                t = hg * PEER_UNROLL + s
                accs = [jnp.zeros((SUBLANES, LANES), F32) for _ in range(8)]
                for e in range(PEER_SLOTS):
                    if e % 2 == 0:
                        start_row(next_row0 + t, 1 - slot, t, half + e // 2)
                    accs[e % 8] = accs[e % 8] + tile_f32(t, e)[SUBLANES:] * wsp_ref[t, e:e + 1, :]
                ffn = ((accs[0] + accs[1]) + (accs[2] + accs[3])) + ((accs[4] + accs[5]) + (accs[6] + accs[7]))
                y = x1_ref[tok0 + t] + gate2_ref[0] * ffn
                ms = jnp.sum(jnp.sum(y * y, axis=1, keepdims=True), axis=0, keepdims=True) * (1.0 / d)
                ys.append(y * lax.rsqrt(ms + EPS) * gf_ref[...])
            for s in range(PEER_UNROLL):
                o_ref[tok0 + hg * PEER_UNROLL + s] = ys[s]
            return c2

        lax.fori_loop(0, PEER_TOK_GROUP // PEER_UNROLL, v_phase, 0)
        return carry

    lax.fori_loop(0, PEER_GROUPS, process, 0)

    @pl.when(step == last_step)
    def _():
        wait(PEER_GROUPS % 2)


def _peer_ffn(idx, gcol, h3, x13, gate2, gf, uv):
    n = h3.shape[0]
    bsz = gate2.shape[0]
    blocks_per_batch = n // bsz // PEER_TOK_BLOCK
    nsteps = n // PEER_TOK_BLOCK
    tok = lambda i: (i, 0, 0)
    const = lambda i: (0, 0)
    blocks = idx.reshape(nsteps, PEER_TOK_BLOCK, PEER_SLOTS)
    nxt = jnp.concatenate([blocks[1:, :PEER_TOK_GROUP], blocks[-1:, :PEER_TOK_GROUP]], axis=0)
    idx_rows = PEER_TOK_BLOCK + PEER_TOK_GROUP
    idx_ext = jnp.concatenate([blocks, nxt], axis=1).reshape(nsteps * idx_rows, PEER_SLOTS)
    gmat = jnp.asarray((np.arange(PEER_SLOTS * SUBLANES)[None, :] // SUBLANES
                        == np.arange(PEER_SLOTS)[:, None]).astype(np.float32)).astype(BF16)
    return pl.pallas_call(
        _peer_ffn_kernel,
        grid=(n // PEER_TOK_BLOCK,),
        in_specs=[pl.BlockSpec((idx_rows, PEER_SLOTS), lambda i: (i, 0), memory_space=pltpu.SMEM),
                  pl.BlockSpec((PEER_GROUPS, PEER_SLOTS, PEER_TOK_GROUP), tok),
                  pl.BlockSpec((PEER_TOK_BLOCK, SUBLANES, LANES), tok),
                  pl.BlockSpec((PEER_TOK_BLOCK, SUBLANES, LANES), tok),
                  pl.BlockSpec((1, SUBLANES, LANES), lambda i: (i // blocks_per_batch, 0, 0)),
                  pl.BlockSpec((SUBLANES, LANES), const),
                  pl.BlockSpec(gmat.shape, const),
                  pl.BlockSpec(memory_space=pl.ANY)],
        out_specs=pl.BlockSpec((PEER_TOK_BLOCK, SUBLANES, LANES), tok),
        out_shape=jax.ShapeDtypeStruct((n, SUBLANES, LANES), F32),
        scratch_shapes=[pltpu.VMEM((2, PEER_SLOT_ROWS, TILE_ROWS, LANES), BF16),
                        pltpu.SemaphoreType.DMA((2,)),
                        pltpu.VMEM((PEER_TOK_GROUP, PEER_SLOTS, LANES), F32)],
        compiler_params=_cparams(("arbitrary",)),
        name="peer_ffn",
    )(idx_ext, gcol, h3, x13, gate2, gf, gmat, uv)


def _rope_tables(seq):
    t = np.arange(seq)
    pos = np.stack([t // GRID_W, t % GRID_W], axis=1).astype(np.float32)
    half = HEAD_DIM // 4
    dlane = np.arange(HEAD_DIM)
    axis = dlane // (HEAD_DIM // 2)
    fidx = dlane % half
    sign = np.where((dlane % (HEAD_DIM // 2)) < half, -1.0, 1.0).astype(np.float32)
    freqs = ROPE_THETA ** (-jnp.arange(half, dtype=F32) / half)
    ang = jnp.asarray(pos)[:, axis] * freqs[fidx][None, :]
    cos = jnp.cos(ang)
    sin = jnp.sin(ang) * sign[None, :]
    return jnp.tile(cos, (1, 2)), jnp.tile(sin, (1, 2))


def _pack_expert_tables(peer_u, peer_v):
    ne = peer_u.shape[0]
    return jnp.concatenate([peer_u.reshape(ne, SUBLANES, LANES), peer_v.reshape(ne, SUBLANES, LANES)],
                           axis=1).astype(BF16)


def kernel(x, c, w_ada, b_ada, norm1_g, w_in, q_norm_g, k_norm_g, natten_rpb, group_norm_a_g,
           group_norm_b_g, w_out, norm2_g, peer_w_query, peer_sub_keys_1, peer_sub_keys_2, peer_u,
           peer_v, final_norm_g):
    bsz, seq, d = x.shape
    depth = w_ada.shape[0]
    n = bsz * seq
    tm = min(512, seq)
    assert seq % tm == 0 and seq % PEER_TOK_BLOCK == 0 and seq % GRID_W == 0

    cos_t, sin_t = _rope_tables(seq)
    bd = jnp.asarray((np.arange(QK_WIDTH)[:, None] // HEAD_DIM
                      == np.arange(QK_WIDTH)[None, :] // HEAD_DIM).astype(np.float32))
    c8 = jnp.pad(c, ((0, 8 - bsz % 8 if bsz % 8 else 0), (0, 0)))

    for l in range(depth):
        mod = _ada(c8, w_ada[l], b_ada[l])[:bsz]
        shift1, scale1, gate1, shift2, scale2, gate2 = [m.reshape(bsz, 1, d) for m in jnp.split(mod, 6, axis=-1)]
        gqk = jnp.concatenate([jnp.tile(q_norm_g[l] * (HEAD_DIM ** -0.5), A_HEADS),
                               jnp.tile(k_norm_g[l], A_KV_HEADS)]).reshape(1, QK_WIDTH)
        qa, kx, vx, qlo, qhi, kb, vb = _inproj(
            x, shift1, scale1, norm1_g[l].reshape(1, d), w_in[l].astype(BF16), bd, gqk, cos_t, sin_t, tm)
        out_a = _gqa(qa, kx, vx, tq=min(256, seq), tk=min(512, seq))
        out_b = _natten(qlo, qhi, kb, vb, _natten_bias_tables(natten_rpb[l], seq))
        x1, h2 = _outproj(out_a, out_b, x, group_norm_a_g[l].reshape(1, A_WIDTH),
                          group_norm_b_g[l].reshape(1, B_WIDTH), w_out[l].astype(BF16), gate1,
                          norm2_g[l].reshape(1, d), scale2, shift2, tm)
        h2f = h2.reshape(n, d)
        idx_t, gate_t = _peer_topk(h2f, peer_w_query[l].astype(BF16), peer_sub_keys_1[l].astype(BF16),
                                   peer_sub_keys_2[l].astype(BF16), tt=min(256, n))
        uv = _pack_expert_tables(peer_u[l], peer_v[l])
        last = l == depth - 1
        assert last, "final norm is fused into the PEER stage; depth > 1 needs an un-normed variant"
        tile = lambda a: a.reshape(a.shape[0], SUBLANES, LANES)
        gcol = gate_t.reshape(PEER_SLOTS, n // PEER_TOK_GROUP, PEER_TOK_GROUP).transpose(1, 0, 2)
        x = _peer_ffn(idx_t.T, gcol, tile(h2f.astype(F32)), tile(x1.reshape(n, d)),
                      tile(gate2.reshape(bsz, d)), final_norm_g.reshape(SUBLANES, LANES),
                      uv).reshape(bsz, seq, d)
    return x
```

```python
import functools

import numpy as np
import jax
import jax.numpy as jnp
from jax import lax
from jax.experimental import pallas as pl
from jax.experimental.pallas import tpu as pltpu

F32 = jnp.float32
BF16 = jnp.bfloat16

GRID_W = 64
HEAD_DIM = 64
A_HEADS = 8
A_KV_HEADS = 2
B_HEADS = 8
A_WIDTH = A_HEADS * HEAD_DIM
A_KV_WIDTH = A_KV_HEADS * HEAD_DIM
B_WIDTH = B_HEADS * HEAD_DIM
QK_WIDTH = A_WIDTH + A_KV_WIDTH
ROPE_THETA = 10000.0
NA_KH = 8
NA_KW = 16
PEER_HEADS = 8
PEER_N_KEYS = 128
PEER_SUB_DIM = 128
PEER_TOPK = 16
PEER_SLOTS = PEER_HEADS * PEER_TOPK
EPS = 1e-6

LANES = 128
Q_BLOCK = 128
NA_SLAB_BLOCKS = 5
NEG_BIG = -1e30
VMEM_LIMIT = 48 * 1024 * 1024


def _cparams(sem):
    return pltpu.CompilerParams(dimension_semantics=sem, vmem_limit_bytes=VMEM_LIMIT)


def _ada_kernel(c_ref, w_ref, b_ref, o_ref):
    c = c_ref[...]
    sc = c / (1.0 + jnp.exp(-c))
    o_ref[...] = jnp.dot(sc, w_ref[...], preferred_element_type=F32) + b_ref[...]


def _ada(c8, w_ada, b_ada):
    rows, d = c8.shape
    n = w_ada.shape[1]
    tn = 512
    return pl.pallas_call(
        _ada_kernel,
        grid=(n // tn,),
        in_specs=[pl.BlockSpec((rows, d), lambda j: (0, 0)),
                  pl.BlockSpec((d, tn), lambda j: (0, j)),
                  pl.BlockSpec((1, tn), lambda j: (0, j))],
        out_specs=pl.BlockSpec((rows, tn), lambda j: (0, j)),
        out_shape=jax.ShapeDtypeStruct((rows, n), F32),
        compiler_params=_cparams(("arbitrary",)),
        name="ada",
    )(c8, w_ada, b_ada.reshape(1, n))


def _inproj_kernel(x_ref, sh_ref, sc_ref, g1_ref, w_ref, bd_ref, gqk_ref, cos_ref, sin_ref,
                   qa_ref, kx_ref, vx_ref, qlo_ref, qhi_ref, kb_ref, vb_ref):
    x = x_ref[0]
    ms = jnp.mean(x * x, axis=-1, keepdims=True)
    h = x * lax.rsqrt(ms + EPS) * g1_ref[...]
    h = h * (1.0 + sc_ref[0]) + sh_ref[0]
    p = jnp.dot(h.astype(BF16), w_ref[...], preferred_element_type=F32)
    tm = p.shape[0]

    lane = lax.broadcasted_iota(jnp.int32, (tm, LANES), 1)
    lo = lane < HEAD_DIM
    first16 = (lane % 32) < 16

    pq = p[:, :QK_WIDTH]
    ss = jnp.dot(pq * pq, bd_ref[...], preferred_element_type=F32)
    qn = pq * lax.rsqrt(ss * (1.0 / HEAD_DIM) + EPS) * gqk_ref[...]
    cos = cos_ref[...]
    sin = sin_ref[...]

    def rope(cn):
        partner = jnp.where(first16, pltpu.roll(cn, LANES - 16, 1), pltpu.roll(cn, 16, 1))
        return cn * cos + partner * sin

    for k in range(A_WIDTH // LANES):
        qa_ref[0, :, k * LANES:(k + 1) * LANES] = rope(qn[:, k * LANES:(k + 1) * LANES]).astype(BF16)
    kk = rope(qn[:, A_WIDTH:QK_WIDTH])
    kr = pltpu.roll(kk, HEAD_DIM, 1)
    zero = jnp.zeros_like(kk)
    kx_ref[0, 0, 0] = jnp.where(lo, kk, zero).astype(BF16)
    kx_ref[0, 0, 1] = jnp.where(lo, zero, kr).astype(BF16)
    kx_ref[0, 1, 0] = jnp.where(lo, kr, zero).astype(BF16)
    kx_ref[0, 1, 1] = jnp.where(lo, zero, kk).astype(BF16)
    vv = p[:, QK_WIDTH:QK_WIDTH + A_KV_WIDTH]
    vr = pltpu.roll(vv, HEAD_DIM, 1)
    vx_ref[0, 0, 0] = jnp.where(lo, vv, zero).astype(BF16)
    vx_ref[0, 0, 1] = jnp.where(lo, zero, vr).astype(BF16)
    vx_ref[0, 1, 0] = jnp.where(lo, vr, zero).astype(BF16)
    vx_ref[0, 1, 1] = jnp.where(lo, zero, vv).astype(BF16)

    o3 = QK_WIDTH + A_KV_WIDTH
    scale = HEAD_DIM ** -0.5
    for k in range(B_WIDTH // LANES):
        cq = p[:, o3 + k * LANES:o3 + (k + 1) * LANES] * scale
        qlo_ref[0, :, k * LANES:(k + 1) * LANES] = jnp.where(lo, cq, zero).astype(BF16)
        qhi_ref[0, :, k * LANES:(k + 1) * LANES] = jnp.where(lo, zero, cq).astype(BF16)
    kb_ref[0] = p[:, o3 + B_WIDTH:o3 + 2 * B_WIDTH].astype(BF16)
    vb_ref[0] = p[:, o3 + 2 * B_WIDTH:o3 + 3 * B_WIDTH].astype(BF16)


def _inproj(x, shift1, scale1, g1, w_in, bd, gqk, cos_t, sin_t, tm):
    bsz, seq, d = x.shape
    n = w_in.shape[1]
    row = lambda b, i: (b, i, 0)
    ext = lambda b, i: (b, 0, 0, i, 0)
    bcast = lambda b, i: (b, 0, 0)
    const = lambda b, i: (0, 0)
    sd = jax.ShapeDtypeStruct
    return pl.pallas_call(
        _inproj_kernel,
        grid=(bsz, seq // tm),
        in_specs=[pl.BlockSpec((1, tm, d), row),
                  pl.BlockSpec((1, 1, d), bcast),
                  pl.BlockSpec((1, 1, d), bcast),
                  pl.BlockSpec((1, d), const),
                  pl.BlockSpec((d, n), const),
                  pl.BlockSpec((QK_WIDTH, QK_WIDTH), const),
                  pl.BlockSpec((1, QK_WIDTH), const),
                  pl.BlockSpec((tm, LANES), lambda b, i: (i, 0)),
                  pl.BlockSpec((tm, LANES), lambda b, i: (i, 0))],
        out_specs=[pl.BlockSpec((1, tm, A_WIDTH), row),
                   pl.BlockSpec((1, 2, 2, tm, LANES), ext),
                   pl.BlockSpec((1, 2, 2, tm, LANES), ext),
                   pl.BlockSpec((1, tm, B_WIDTH), row),
                   pl.BlockSpec((1, tm, B_WIDTH), row),
                   pl.BlockSpec((1, tm, B_WIDTH), row),
                   pl.BlockSpec((1, tm, B_WIDTH), row)],
        out_shape=[sd((bsz, seq, A_WIDTH), BF16),
                   sd((bsz, 2, 2, seq, LANES), BF16),
                   sd((bsz, 2, 2, seq, LANES), BF16),
                   sd((bsz, seq, B_WIDTH), BF16),
                   sd((bsz, seq, B_WIDTH), BF16),
                   sd((bsz, seq, B_WIDTH), BF16),
                   sd((bsz, seq, B_WIDTH), BF16)],
        compiler_params=_cparams(("arbitrary", "arbitrary")),
        name="inproj",
    )(x, shift1, scale1, g1, w_in, bd, gqk, cos_t, sin_t)


def _gqa_kernel(q_ref, k_ref, v_ref, o_ref, *, tk):
    tq = q_ref.shape[1]
    seq = k_ref.shape[3]
    rows = 2 * tq
    lane = lax.broadcasted_iota(jnp.int32, (rows, LANES), 1)
    lo = lane < HEAD_DIM
    nt = (((1,), (1,)), ((), ()))
    qc = jnp.concatenate([q_ref[0, :, :LANES], q_ref[0, :, LANES:]], axis=0)

    def body(c, carry):
        m0, l0, m1, l1, acc = carry
        start = pl.multiple_of(c * tk, tk)
        k0 = k_ref[0, 0, 0, pl.ds(start, tk), :]
        k1 = k_ref[0, 0, 1, pl.ds(start, tk), :]
        s0 = lax.dot_general(qc, k0, nt, preferred_element_type=F32)
        s1 = lax.dot_general(qc, k1, nt, preferred_element_type=F32)
        m0n = jnp.maximum(m0, jnp.max(s0, axis=-1, keepdims=True))
        m1n = jnp.maximum(m1, jnp.max(s1, axis=-1, keepdims=True))
        a0 = jnp.exp(m0 - m0n)
        a1 = jnp.exp(m1 - m1n)
        p0 = jnp.exp(s0 - m0n)
        p1 = jnp.exp(s1 - m1n)
        l0 = a0 * l0 + jnp.sum(p0, axis=-1, keepdims=True)
        l1 = a1 * l1 + jnp.sum(p1, axis=-1, keepdims=True)
        v0 = v_ref[0, 0, 0, pl.ds(start, tk), :]
        v1 = v_ref[0, 0, 1, pl.ds(start, tk), :]
        pv = (jnp.dot(p0.astype(BF16), v0, preferred_element_type=F32)
              + jnp.dot(p1.astype(BF16), v1, preferred_element_type=F32))
        acc = acc * jnp.where(lo, a0, a1) + pv
        return m0n, l0, m1n, l1, acc

    init = (jnp.full((rows, 1), NEG_BIG, F32), jnp.zeros((rows, 1), F32),
            jnp.full((rows, 1), NEG_BIG, F32), jnp.zeros((rows, 1), F32),
            jnp.zeros((rows, LANES), F32))
    nkv = seq // tk
    m0, l0, m1, l1, acc = lax.fori_loop(0, nkv, body, init, unroll=4 if nkv % 4 == 0 else 1)
    out = (acc / jnp.where(lo, l0, l1)).astype(BF16)
    o_ref[0, :, :LANES] = out[:tq]
    o_ref[0, :, LANES:] = out[tq:]


def _gqa(qa, kx, vx, tq, tk):
    bsz, seq, _ = qa.shape
    kv = lambda b, g, i: (b, g, 0, 0, 0)
    return pl.pallas_call(
        functools.partial(_gqa_kernel, tk=tk),
        grid=(bsz, A_KV_HEADS, seq // tq),
        in_specs=[pl.BlockSpec((1, tq, 2 * LANES), lambda b, g, i: (b, i, g)),
                  pl.BlockSpec((1, 1, 2, seq, LANES), kv),
                  pl.BlockSpec((1, 1, 2, seq, LANES), kv)],
        out_specs=pl.BlockSpec((1, tq, 2 * LANES), lambda b, g, i: (b, i, g)),
        out_shape=jax.ShapeDtypeStruct((bsz, seq, A_WIDTH), BF16),
        compiler_params=_cparams(("arbitrary", "arbitrary", "arbitrary")),
        name="gqa",
    )(qa, kx, vx)


def _natten_kernel(qlo_ref, qhi_ref, k0, k1, k2, k3, k4, v0, v1, v2, v3, v4, bias_ref, o_ref):
    ks = jnp.concatenate([k0[0], k1[0], k2[0], k3[0], k4[0]], axis=0)
    vs = jnp.concatenate([v0[0], v1[0], v2[0], v3[0], v4[0]], axis=0)
    lane = lax.broadcasted_iota(jnp.int32, (Q_BLOCK, LANES), 1)
    lo = lane < HEAD_DIM
    nt = (((1,), (1,)), ((), ()))
    for j in range(B_WIDTH // LANES):
        sl = slice(j * LANES, (j + 1) * LANES)
        kc = ks[:, sl]
        vc = vs[:, sl]
        outs = []
        for half, q_ref in enumerate((qlo_ref, qhi_ref)):
            s = lax.dot_general(q_ref[0, :, sl], kc, nt, preferred_element_type=F32)
            s = s + bias_ref[0, 2 * j + half]
            m = jnp.max(s, axis=-1, keepdims=True)
            p = jnp.exp(s - m)
            l = jnp.sum(p, axis=-1, keepdims=True)
            outs.append(jnp.dot(p.astype(BF16), vc, preferred_element_type=F32) / l)
        o_ref[0, :, sl] = jnp.where(lo, outs[0], outs[1]).astype(BF16)


def _natten(qlo, qhi, kb, vb, bias):
    bsz, seq, _ = qlo.shape
    nblk = seq // Q_BLOCK
    qspec = pl.BlockSpec((1, Q_BLOCK, B_WIDTH), lambda b, j: (b, j, 0))

    def slab(i):
        return pl.BlockSpec((1, Q_BLOCK, B_WIDTH),
                            lambda b, j: (b, jnp.clip(j - 2, 0, nblk - NA_SLAB_BLOCKS) + i, 0))

    def btype(b, j):
        t = jnp.where(j < 2, j, jnp.where(j >= nblk - 2, j - (nblk - NA_SLAB_BLOCKS), 2))
        return (t, 0, 0, 0)

    kspecs = [slab(i) for i in range(NA_SLAB_BLOCKS)]
    return pl.pallas_call(
        _natten_kernel,
        grid=(bsz, nblk),
        in_specs=[qspec, qspec] + kspecs + kspecs +
                 [pl.BlockSpec((1, B_HEADS, Q_BLOCK, NA_SLAB_BLOCKS * Q_BLOCK), btype)],
        out_specs=pl.BlockSpec((1, Q_BLOCK, B_WIDTH), lambda b, j: (b, j, 0)),
        out_shape=jax.ShapeDtypeStruct((bsz, seq, B_WIDTH), BF16),
        compiler_params=_cparams(("arbitrary", "arbitrary")),
        name="natten",
    )(qlo, qhi, *([kb] * NA_SLAB_BLOCKS), *([vb] * NA_SLAB_BLOCKS), bias)


def _natten_bias_tables(rpb, seq):
    rows = seq // GRID_W
    nblk = seq // Q_BLOCK
    assert rows >= 2 * NA_SLAB_BLOCKS and NA_KH <= rows
    reps = np.array([0, 1, 2, nblk - 2, nblk - 1])
    qrows = Q_BLOCK // GRID_W
    srows = NA_SLAB_BLOCKS * qrows
    r = reps[:, None] * qrows + np.arange(qrows)[None, :]
    rs = np.clip(r - NA_KH // 2, 0, rows - NA_KH)
    key_r = (np.clip(reps - 2, 0, nblk - NA_SLAB_BLOCKS) * qrows)[:, None] + np.arange(srows)[None, :]
    valid_r = (key_r[:, None, :] >= rs[:, :, None]) & (key_r[:, None, :] < rs[:, :, None] + NA_KH)
    dri = np.clip(key_r[:, None, :] - r[:, :, None] + (NA_KH - 1), 0, 2 * NA_KH - 2)
    c = np.arange(GRID_W)
    cs = np.clip(c - NA_KW // 2, 0, GRID_W - NA_KW)
    valid_c = (c[None, :] >= cs[:, None]) & (c[None, :] < cs[:, None] + NA_KW)
    dci = np.clip(c[None, :] - c[:, None] + (NA_KW - 1), 0, 2 * NA_KW - 2)
    oh_r = jnp.asarray((dri[..., None] == np.arange(2 * NA_KH - 1)).astype(np.float32))
    oh_c = jnp.asarray((dci[..., None] == np.arange(2 * NA_KW - 1)).astype(np.float32))
    hp = lax.Precision.HIGHEST
    t1 = jnp.einsum('zrka,hab->zhrkb', oh_r, rpb, precision=hp)
    bias = jnp.einsum('zhrkb,qcb->zhrqkc', t1, oh_c, precision=hp)
    valid = valid_r[:, None, :, None, :, None] & valid_c[None, None, None, :, None, :]
    bias = jnp.where(jnp.asarray(valid), bias, NEG_BIG)
    return bias.reshape(len(reps), rpb.shape[0], Q_BLOCK, NA_SLAB_BLOCKS * Q_BLOCK).astype(F32)


def _outproj_kernel(oa_ref, ob_ref, x_ref, ga_ref, gb_ref, w_ref, gate_ref, g2_ref, sc_ref, sh_ref,
                    x1_ref, h2_ref):
    def gnorm(o, g):
        o = o.astype(F32)
        return o * lax.rsqrt(jnp.mean(o * o, axis=-1, keepdims=True) + EPS) * g

    na = gnorm(oa_ref[0], ga_ref[...]).astype(BF16)
    nb = gnorm(ob_ref[0], gb_ref[...]).astype(BF16)
    mix = (jnp.dot(na, w_ref[:A_WIDTH, :], preferred_element_type=F32)
           + jnp.dot(nb, w_ref[A_WIDTH:, :], preferred_element_type=F32))
    x1 = x_ref[0] + gate_ref[0] * mix
    x1_ref[0] = x1
    h2 = x1 * lax.rsqrt(jnp.mean(x1 * x1, axis=-1, keepdims=True) + EPS) * g2_ref[...]
    h2_ref[0] = (h2 * (1.0 + sc_ref[0]) + sh_ref[0]).astype(BF16)


def _outproj(oa, ob, x, ga, gb, w_out, gate1, g2, scale2, shift2, tm):
    bsz, seq, d = x.shape
    row = lambda b, i: (b, i, 0)
    bcast = lambda b, i: (b, 0, 0)
    const = lambda b, i: (0, 0)
    return pl.pallas_call(
        _outproj_kernel,
        grid=(bsz, seq // tm),
        in_specs=[pl.BlockSpec((1, tm, A_WIDTH), row),
                  pl.BlockSpec((1, tm, B_WIDTH), row),
                  pl.BlockSpec((1, tm, d), row),
                  pl.BlockSpec((1, A_WIDTH), const),
                  pl.BlockSpec((1, B_WIDTH), const),
                  pl.BlockSpec((A_WIDTH + B_WIDTH, d), const),
                  pl.BlockSpec((1, 1, d), bcast),
                  pl.BlockSpec((1, d), const),
                  pl.BlockSpec((1, 1, d), bcast),
                  pl.BlockSpec((1, 1, d), bcast)],
        out_specs=[pl.BlockSpec((1, tm, d), row), pl.BlockSpec((1, tm, d), row)],
        out_shape=[jax.ShapeDtypeStruct((bsz, seq, d), F32),
                   jax.ShapeDtypeStruct((bsz, seq, d), BF16)],
        compiler_params=_cparams(("arbitrary", "arbitrary")),
        name="outproj",
    )(oa, ob, x, ga, gb, w_out, gate1, g2, scale2, shift2)


def _top16_axis0(s, order=None, payload=None):
    if order is None:
        order = lax.broadcasted_iota(jnp.int32, s.shape, 0)
    big = jnp.int32(2 ** 30)
    vals, picks = [], []
    for _ in range(PEER_TOPK):
        m = jnp.max(s, axis=0, keepdims=True)
        first = jnp.min(jnp.where(s == m, order, big), axis=0, keepdims=True)
        hit = order == first
        vals.append(m)
        if payload is None:
            picks.append(first)
        else:
            picks.append(jnp.sum(jnp.where(hit, payload, 0), axis=0, keepdims=True))
        s = jnp.where(hit, -jnp.inf, s)
    return jnp.concatenate(vals, axis=0), jnp.concatenate(picks, axis=0)


def _pair_candidates(v1, v2, i1, i2):
    t = v1.shape[1]
    r16 = lax.broadcasted_iota(jnp.int32, (PEER_TOPK, t), 0)
    neg = -jnp.inf
    pieces = [
        (v1[0:1] + v2, r16, i1[0:1] * PEER_N_KEYS + i2, None),
        (v1[1:2] + v2, PEER_TOPK + r16, i1[1:2] * PEER_N_KEYS + i2, r16 < 8),
        (v1 + v2[0:1], r16 * PEER_TOPK, i1 * PEER_N_KEYS + i2[0:1], r16 >= 2),
    ]
    for b, amax in ((1, 7), (2, 4), (3, 3), (4, 2)):
        pieces.append((v1 + v2[b:b + 1], r16 * PEER_TOPK + b, i1 * PEER_N_KEYS + i2[b:b + 1],
                       jnp.where(r16 >= 2, r16, PEER_TOPK) <= amax))
    vals = jnp.concatenate([p[0] if p[3] is None else jnp.where(p[3], p[0], neg) for p in pieces], axis=0)
    rank = jnp.concatenate([p[1] if p[3] is None else jnp.where(p[3], p[1], jnp.int32(2 ** 30))
                            for p in pieces], axis=0)
    eid = jnp.concatenate([p[2] for p in pieces], axis=0)
    return vals, rank, eid


def _peer_topk_kernel(h_ref, wq_ref, sk1_ref, sk2_ref, idx_ref, gate_ref):
    q = jnp.dot(h_ref[...], wq_ref[...], preferred_element_type=F32)
    nt = (((1,), (1,)), ((), ()))
    for hh in range(PEER_HEADS):
        base = 2 * hh * PEER_SUB_DIM
        q1 = q[:, base:base + PEER_SUB_DIM].astype(BF16)
        q2 = q[:, base + PEER_SUB_DIM:base + 2 * PEER_SUB_DIM].astype(BF16)
        s1 = lax.dot_general(sk1_ref[...], q1, nt, preferred_element_type=F32)
        s2 = lax.dot_general(sk2_ref[...], q2, nt, preferred_element_type=F32)
        v1, i1 = _top16_axis0(s1)
        v2, i2 = _top16_axis0(s2)
        cand, rank, cidx = _pair_candidates(v1, v2, i1, i2)
        top, eidx = _top16_axis0(cand, order=rank, payload=cidx)
        e = jnp.exp(top - top[0:1])
        gate = e / jnp.sum(e, axis=0, keepdims=True)
        idx_ref[hh * PEER_TOPK:(hh + 1) * PEER_TOPK, :] = eidx
        gate_ref[hh * PEER_TOPK:(hh + 1) * PEER_TOPK, :] = gate


def _peer_topk(h2, wq, sk1, sk2, tt):
    n, d = h2.shape
    const = lambda i: (0, 0)
    return pl.pallas_call(
        _peer_topk_kernel,
        grid=(n // tt,),
        in_specs=[pl.BlockSpec((tt, d), lambda i: (i, 0)),
                  pl.BlockSpec(wq.shape, const),
                  pl.BlockSpec(sk1.shape, const),
                  pl.BlockSpec(sk2.shape, const)],
        out_specs=[pl.BlockSpec((PEER_SLOTS, tt), lambda i: (0, i)),
                   pl.BlockSpec((PEER_SLOTS, tt), lambda i: (0, i))],
        out_shape=[jax.ShapeDtypeStruct((PEER_SLOTS, n), jnp.int32),
                   jax.ShapeDtypeStruct((PEER_SLOTS, n), F32)],
        compiler_params=_cparams(("arbitrary",)),
        name="peer_topk",
    )(h2, wq, sk1, sk2)


PEER_TOK_BLOCK = 256
PEER_TOK_GROUP = 8
PEER_GROUPS = PEER_TOK_BLOCK // PEER_TOK_GROUP
PEER_SLOT_ROWS = PEER_TOK_GROUP * PEER_SLOTS
SUBLANES = 8
TILE_ROWS = 2 * SUBLANES
PEER_UNROLL = 4
PEER_CHUNK = 16
PEER_NCHUNK = PEER_SLOTS // PEER_CHUNK


def _peer_ffn_kernel(idx_ref, gcol_ref, h_ref, x1_ref, gate2_ref, gf_ref, gmat_ref, uv_ref,
                     o_ref, buf_ref, sem_ref, wsp_ref):
    step = pl.program_id(0)
    last_step = pl.num_programs(0) - 1
    d = SUBLANES * LANES
    lane = lax.broadcasted_iota(jnp.int32, (PEER_SLOTS, LANES), 1)

    def start_row(row, slot, t, e):
        r = idx_ref[row, e]
        pltpu.make_async_copy(uv_ref.at[r], buf_ref.at[slot, t * PEER_SLOTS + e],
                              sem_ref.at[slot]).start(priority=e % 2)

    def wait(slot):
        pltpu.make_async_copy(uv_ref.at[pl.ds(0, PEER_SLOT_ROWS)], buf_ref.at[slot],
                              sem_ref.at[slot]).wait()

    @pl.when(step == 0)
    def _():
        def tok(t, carry):
            for e in range(PEER_SLOTS):
                start_row(t, 0, t, e)
            return carry
        lax.fori_loop(0, PEER_TOK_GROUP, tok, 0)

    def process(g, carry):
        slot = g % 2
        wait(slot)
        tok0 = pl.multiple_of(g * PEER_TOK_GROUP, PEER_TOK_GROUP)
        next_row0 = tok0 + PEER_TOK_GROUP
        half = PEER_SLOTS // 2

        def tile_f32(t, e):
            return buf_ref[slot, t * PEER_SLOTS + e].astype(F32)

        def u_phase(hg, a):
            for s in range(PEER_UNROLL):
                t = hg * PEER_UNROLL + s
                hv = h_ref[tok0 + t]
                part = jnp.zeros((PEER_SLOTS, LANES), F32)
                for c in range(PEER_NCHUNK):
                    prods = []
                    for k in range(PEER_CHUNK):
                        e = c * PEER_CHUNK + k
                        if k % 2 == 0:
                            start_row(next_row0 + t, 1 - slot, t, e // 2)
                        prods.append(tile_f32(t, e)[:SUBLANES] * hv)
                    prod = jnp.concatenate(prods, axis=0).astype(BF16)
                    part = part + jnp.dot(gmat_ref[:, c * LANES:(c + 1) * LANES], prod,
                                          preferred_element_type=F32)
                a = jnp.where(lane == t, jnp.sum(part, axis=1, keepdims=True), a)
            return a

        a = lax.fori_loop(0, PEER_TOK_GROUP // PEER_UNROLL, u_phase, jnp.zeros((PEER_SLOTS, LANES), F32))
        act = 0.5 * a * (1.0 + lax.erf(a * (2.0 ** -0.5)))
        w8 = act[:, :PEER_TOK_GROUP] * gcol_ref[g]
        for t in range(PEER_TOK_GROUP):
            wsp_ref[t] = jnp.broadcast_to(w8[:, t:t + 1], (PEER_SLOTS, LANES))

        def v_phase(hg, c2):
            ys = []
            for s in range(PEER_UNROLL):
                t = hg * PEER_UNROLL + s
                accs = [jnp.zeros((SUBLANES, LANES), F32) for _ in range(8)]
                for e in range(PEER_SLOTS):
                    if e % 2 == 0:
                        start_row(next_row0 + t, 1 - slot, t, half + e // 2)
                    accs[e % 8] = accs[e % 8] + tile_f32(t, e)[SUBLANES:] * wsp_ref[t, e:e + 1, :]
                ffn = ((accs[0] + accs[1]) + (accs[2] + accs[3])) + ((accs[4] + accs[5]) + (accs[6] + accs[7]))
                y = x1_ref[tok0 + t] + gate2_ref[0] * ffn
                ms = jnp.sum(jnp.sum(y * y, axis=1, keepdims=True), axis=0, keepdims=True) * (1.0 / d)
                ys.append(y * lax.rsqrt(ms + EPS) * gf_ref[...])
            for s in range(PEER_UNROLL):
                o_ref[tok0 + hg * PEER_UNROLL + s] = ys[s]
            return c2

        lax.fori_loop(0, PEER_TOK_GROUP // PEER_UNROLL, v_phase, 0)
        return carry

    lax.fori_loop(0, PEER_GROUPS, process, 0)

    @pl.when(step == last_step)
    def _():
        wait(PEER_GROUPS % 2)


def _peer_ffn(idx, gcol, h3, x13, gate2, gf, uv):
    n = h3.shape[0]
    bsz = gate2.shape[0]
    blocks_per_batch = n // bsz // PEER_TOK_BLOCK
    nsteps = n // PEER_TOK_BLOCK
    tok = lambda i: (i, 0, 0)
    const = lambda i: (0, 0)
    blocks = idx.reshape(nsteps, PEER_TOK_BLOCK, PEER_SLOTS)
    nxt = jnp.concatenate([blocks[1:, :PEER_TOK_GROUP], blocks[-1:, :PEER_TOK_GROUP]], axis=0)
    idx_rows = PEER_TOK_BLOCK + PEER_TOK_GROUP
    idx_ext = jnp.concatenate([blocks, nxt], axis=1).reshape(nsteps * idx_rows, PEER_SLOTS)
    gmat = jnp.asarray((np.arange(PEER_SLOTS * SUBLANES)[None, :] // SUBLANES
                        == np.arange(PEER_SLOTS)[:, None]).astype(np.float32)).astype(BF16)
    return pl.pallas_call(
        _peer_ffn_kernel,
        grid=(n // PEER_TOK_BLOCK,),
        in_specs=[pl.BlockSpec((idx_rows, PEER_SLOTS), lambda i: (i, 0), memory_space=pltpu.SMEM),
                  pl.BlockSpec((PEER_GROUPS, PEER_SLOTS, PEER_TOK_GROUP), tok),
                  pl.BlockSpec((PEER_TOK_BLOCK, SUBLANES, LANES), tok),
                  pl.BlockSpec((PEER_TOK_BLOCK, SUBLANES, LANES), tok),
                  pl.BlockSpec((1, SUBLANES, LANES), lambda i: (i // blocks_per_batch, 0, 0)),
                  pl.BlockSpec((SUBLANES, LANES), const),
                  pl.BlockSpec(gmat.shape, const),
                  pl.BlockSpec(memory_space=pl.ANY)],
        out_specs=pl.BlockSpec((PEER_TOK_BLOCK, SUBLANES, LANES), tok),
        out_shape=jax.ShapeDtypeStruct((n, SUBLANES, LANES), F32),
        scratch_shapes=[pltpu.VMEM((2, PEER_SLOT_ROWS, TILE_ROWS, LANES), BF16),
                        pltpu.SemaphoreType.DMA((2,)),
                        pltpu.VMEM((PEER_TOK_GROUP, PEER_SLOTS, LANES), F32)],
        compiler_params=_cparams(("arbitrary",)),
        name="peer_ffn",
    )(idx_ext, gcol, h3, x13, gate2, gf, gmat, uv)


def _rope_tables(seq):
    t = np.arange(seq)
    pos = np.stack([t // GRID_W, t % GRID_W], axis=1).astype(np.float32)
    half = HEAD_DIM // 4
    dlane = np.arange(HEAD_DIM)
    axis = dlane // (HEAD_DIM // 2)
    fidx = dlane % half
    sign = np.where((dlane % (HEAD_DIM // 2)) < half, -1.0, 1.0).astype(np.float32)
    freqs = ROPE_THETA ** (-jnp.arange(half, dtype=F32) / half)
    ang = jnp.asarray(pos)[:, axis] * freqs[fidx][None, :]
    cos = jnp.cos(ang)
    sin = jnp.sin(ang) * sign[None, :]
    return jnp.tile(cos, (1, 2)), jnp.tile(sin, (1, 2))


def _pack_expert_tables(peer_u, peer_v):
    ne = peer_u.shape[0]
    return jnp.concatenate([peer_u.reshape(ne, SUBLANES, LANES), peer_v.reshape(ne, SUBLANES, LANES)],
                           axis=1).astype(BF16)


def kernel(x, c, w_ada, b_ada, norm1_g, w_in, q_norm_g, k_norm_g, natten_rpb, group_norm_a_g,
           group_norm_b_g, w_out, norm2_g, peer_w_query, peer_sub_keys_1, peer_sub_keys_2, peer_u,
           peer_v, final_norm_g):
    bsz, seq, d = x.shape
    depth = w_ada.shape[0]
    n = bsz * seq
    tm = min(512, seq)
    assert seq % tm == 0 and seq % PEER_TOK_BLOCK == 0 and seq % GRID_W == 0

    cos_t, sin_t = _rope_tables(seq)
    bd = jnp.asarray((np.arange(QK_WIDTH)[:, None] // HEAD_DIM
                      == np.arange(QK_WIDTH)[None, :] // HEAD_DIM).astype(np.float32))
    c8 = jnp.pad(c, ((0, 8 - bsz % 8 if bsz % 8 else 0), (0, 0)))

    for l in range(depth):
        mod = _ada(c8, w_ada[l], b_ada[l])[:bsz]
        shift1, scale1, gate1, shift2, scale2, gate2 = [m.reshape(bsz, 1, d) for m in jnp.split(mod, 6, axis=-1)]
        gqk = jnp.concatenate([jnp.tile(q_norm_g[l] * (HEAD_DIM ** -0.5), A_HEADS),
                               jnp.tile(k_norm_g[l], A_KV_HEADS)]).reshape(1, QK_WIDTH)
        qa, kx, vx, qlo, qhi, kb, vb = _inproj(
            x, shift1, scale1, norm1_g[l].reshape(1, d), w_in[l].astype(BF16), bd, gqk, cos_t, sin_t, tm)
        out_a = _gqa(qa, kx, vx, tq=min(256, seq), tk=min(1024, seq))
        out_b = _natten(qlo, qhi, kb, vb, _natten_bias_tables(natten_rpb[l], seq))
        x1, h2 = _outproj(out_a, out_b, x, group_norm_a_g[l].reshape(1, A_WIDTH),
                          group_norm_b_g[l].reshape(1, B_WIDTH), w_out[l].astype(BF16), gate1,
                          norm2_g[l].reshape(1, d), scale2, shift2, tm)
        h2f = h2.reshape(n, d)
        idx_t, gate_t = _peer_topk(h2f, peer_w_query[l].astype(BF16), peer_sub_keys_1[l].astype(BF16),
                                   peer_sub_keys_2[l].astype(BF16), tt=min(256, n))
        uv = _pack_expert_tables(peer_u[l], peer_v[l])
        last = l == depth - 1
        assert last, "final norm is fused into the PEER stage; depth > 1 needs an un-normed variant"
        tile = lambda a: a.reshape(a.shape[0], SUBLANES, LANES)
        gcol = gate_t.reshape(PEER_SLOTS, n // PEER_TOK_GROUP, PEER_TOK_GROUP).transpose(1, 0, 2)
        x = _peer_ffn(idx_t.T, gcol, tile(h2f.astype(F32)), tile(x1.reshape(n, d)),
                      tile(gate2.reshape(bsz, d)), final_norm_g.reshape(SUBLANES, LANES),
                      uv).reshape(bsz, seq, d)
    return x
```
